```python
import jax, jax.numpy as jnp
from jax import lax
import numpy as np

D_MODEL = 1024
BATCH = 2
SEQ = 8192
DEPTH = 1

CONV_WIDTH = D_MODEL
CONV_SIZE = 31
SSM_WIDTH = D_MODEL // 2
SSM_GROUP = 16
SSM_GROUPS = SSM_WIDTH // SSM_GROUP
SSM_STATE = 64
DT_MIN = 0.001
DT_MAX = 0.1
RMS_EPS = 1e-6
LN_EPS = 1e-5
IN_SPLITS = (2 * CONV_WIDTH, CONV_WIDTH, SSM_WIDTH, SSM_WIDTH, D_MODEL, D_MODEL)
IN_WIDTH = sum(IN_SPLITS)
IN_OFFSETS = [int(v) for v in np.cumsum(IN_SPLITS)[:-1]]

kernel_name = "hybrid_conformer_conv_s5_gated_block"


def rms_norm(x, g):
    xf = x.astype(jnp.float32)
    y = xf * lax.rsqrt(jnp.mean(xf * xf, axis=-1, keepdims=True) + RMS_EPS)
    return (y * g.astype(jnp.float32)).astype(x.dtype)


def layer_norm(x, g, b):
    xf = x.astype(jnp.float32)
    mu = jnp.mean(xf, axis=-1, keepdims=True)
    xc = xf - mu
    var = jnp.mean(xc * xc, axis=-1, keepdims=True)
    y = xc * lax.rsqrt(var + LN_EPS) * g.astype(jnp.float32) + b.astype(jnp.float32)
    return y.astype(x.dtype)


def causal_depthwise_conv(u, w, b):
    c = u.shape[-1]
    y = lax.conv_general_dilated(
        u, w[:, None, :].astype(u.dtype), window_strides=(1,),
        padding=[(CONV_SIZE - 1, 0)],
        dimension_numbers=("NWC", "WIO", "NWC"),
        feature_group_count=c)
    return y + b


def s5_mimo(u, lam_re, lam_im, log_dt, b_re, b_im, c_re, c_im, d):
    bsz, length, _ = u.shape
    f32 = jnp.float32
    uf = u.astype(f32).reshape(bsz, length, SSM_GROUPS, SSM_GROUP)
    dt = jnp.exp(log_dt.astype(f32))[:, None]
    lr = lam_re.astype(f32)
    li = lam_im.astype(f32)
    mag = jnp.exp(lr * dt)
    ar = mag * jnp.cos(li * dt)
    ai = mag * jnp.sin(li * dt)
    den = lr * lr + li * li
    zr = ((ar - 1.0) * lr + ai * li) / den
    zi = (ai * lr - (ar - 1.0) * li) / den
    br = b_re.astype(f32)
    bi = b_im.astype(f32)
    bbar_re = zr[..., None] * br - zi[..., None] * bi
    bbar_im = zr[..., None] * bi + zi[..., None] * br
    bu_re = jnp.einsum("blgh,gph->blgp", uf, bbar_re)
    bu_im = jnp.einsum("blgh,gph->blgp", uf, bbar_im)
    a_re = jnp.broadcast_to(ar, bu_re.shape)
    a_im = jnp.broadcast_to(ai, bu_re.shape)

    def combine(e1, e2):
        a1r, a1i, b1r, b1i = e1
        a2r, a2i, b2r, b2i = e2
        return (a2r * a1r - a2i * a1i,
                a2r * a1i + a2i * a1r,
                a2r * b1r - a2i * b1i + b2r,
                a2r * b1i + a2i * b1r + b2i)

    _, _, s_re, s_im = lax.associative_scan(combine, (a_re, a_im, bu_re, bu_im), axis=1)
    y = (jnp.einsum("blgp,ghp->blgh", s_re, c_re.astype(f32))
         - jnp.einsum("blgp,ghp->blgh", s_im, c_im.astype(f32))
         + d.astype(f32) * uf)
    return y.reshape(bsz, length, SSM_WIDTH).astype(u.dtype)


def hybrid_layer(x, pre_g, w_in, conv_w, conv_b, conv_ln_g, conv_ln_b, w_conv_out,
                 lam_re, lam_im, log_dt, b_re, b_im, c_re, c_im, d,
                 w_glu, b_glu, w_ssm_out, w_out, post_g):
    h = rms_norm(x, pre_g)
    proj = jnp.einsum("bld,de->ble", h, w_in)
    conv_in, z_c, u_s, z_s, g_c, g_s = jnp.split(proj, IN_OFFSETS, axis=-1)
    ca, cb = jnp.split(conv_in, 2, axis=-1)
    cu = ca * jax.nn.sigmoid(cb)
    cu = causal_depthwise_conv(cu, conv_w, conv_b)
    cu = jax.nn.silu(layer_norm(cu, conv_ln_g, conv_ln_b))
    conv_out = jnp.einsum("blc,cd->bld", cu * jax.nn.silu(z_c), w_conv_out)
    y = jax.nn.gelu(s5_mimo(u_s, lam_re, lam_im, log_dt, b_re, b_im, c_re, c_im, d))
    y = y * jax.nn.sigmoid(jnp.einsum("blc,ce->ble", y, w_glu) + b_glu)
    ssm_out = jnp.einsum("blc,cd->bld", y * jax.nn.silu(z_s), w_ssm_out)
    merged = jax.nn.sigmoid(g_c) * conv_out + jax.nn.sigmoid(g_s) * ssm_out
    out = jnp.einsum("bld,de->ble", merged, w_out)
    return x + rms_norm(out, post_g)


def setup_inputs(seed: int = 0) -> dict:
    key = jax.random.key(seed)
    ks = jax.random.split(key, 24)
    f32 = jnp.float32
    nrm = lambda k, shape, scale: jax.random.normal(k, shape, f32) * scale
    L, G, P, H = DEPTH, SSM_GROUPS, SSM_STATE, SSM_GROUP
    n = jnp.arange(P, dtype=f32)
    log_dt = (jnp.log(DT_MIN) + jax.random.uniform(ks[11], (L, G), f32)
              * (jnp.log(DT_MAX) - jnp.log(DT_MIN)))
    return {
        "x": jax.random.normal(ks[0], (BATCH, SEQ, D_MODEL), f32),
        "pre_norm_gain": 1.0 + nrm(ks[1], (L, D_MODEL), 0.05),
        "w_in": nrm(ks[2], (L, D_MODEL, IN_WIDTH), D_MODEL ** -0.5),
        "conv_w": nrm(ks[3], (L, CONV_SIZE, CONV_WIDTH), CONV_SIZE ** -0.5),
        "conv_b": nrm(ks[4], (L, CONV_WIDTH), 0.02),
        "conv_ln_gain": 1.0 + nrm(ks[5], (L, CONV_WIDTH), 0.05),
        "conv_ln_bias": nrm(ks[6], (L, CONV_WIDTH), 0.02),
        "w_conv_out": nrm(ks[7], (L, CONV_WIDTH, D_MODEL), CONV_WIDTH ** -0.5),
        "ssm_lambda_re": -0.5 + nrm(ks[8], (L, G, P), 0.01),
        "ssm_lambda_im": jnp.pi * n + nrm(ks[9], (L, G, P), 0.01),
        "ssm_log_dt": log_dt,
        "ssm_b_re": nrm(ks[12], (L, G, P, H), (2.0 * H) ** -0.5),
        "ssm_b_im": nrm(ks[13], (L, G, P, H), (2.0 * H) ** -0.5),
        "ssm_c_re": nrm(ks[14], (L, G, H, P), (2.0 * P) ** -0.5),
        "ssm_c_im": nrm(ks[15], (L, G, H, P), (2.0 * P) ** -0.5),
        "ssm_d": nrm(ks[16], (L, G, H), 1.0),
        "w_ssm_glu": nrm(ks[17], (L, SSM_WIDTH, SSM_WIDTH), SSM_WIDTH ** -0.5),
        "b_ssm_glu": nrm(ks[18], (L, SSM_WIDTH), 0.02),
        "w_ssm_out": nrm(ks[19], (L, SSM_WIDTH, D_MODEL), SSM_WIDTH ** -0.5),
        "w_out": nrm(ks[20], (L, D_MODEL, D_MODEL), D_MODEL ** -0.5),
        "post_norm_gain": 1.0 + nrm(ks[21], (L, D_MODEL), 0.05),
    }


def reference(x, pre_norm_gain, w_in, conv_w, conv_b, conv_ln_gain, conv_ln_bias, w_conv_out,
              ssm_lambda_re, ssm_lambda_im, ssm_log_dt, ssm_b_re, ssm_b_im, ssm_c_re, ssm_c_im,
              ssm_d, w_ssm_glu, b_ssm_glu, w_ssm_out, w_out, post_norm_gain):
    for l in range(DEPTH):
        x = hybrid_layer(x, pre_norm_gain[l], w_in[l], conv_w[l], conv_b[l], conv_ln_gain[l],
                         conv_ln_bias[l], w_conv_out[l], ssm_lambda_re[l], ssm_lambda_im[l],
                         ssm_log_dt[l], ssm_b_re[l], ssm_b_im[l], ssm_c_re[l], ssm_c_im[l],
                         ssm_d[l], w_ssm_glu[l], b_ssm_glu[l], w_ssm_out[l], w_out[l],
                         post_norm_gain[l])
    return x
```

```python
import functools
import math

import jax
import jax.numpy as jnp
from jax import lax
from jax.experimental import pallas as pl
from jax.experimental.pallas import tpu as pltpu

RMS_EPS = 1e-6
LN_EPS = 1e-5
CONV_SIZE = 31
SSM_GROUP = 16
SSM_STATE = 64
CHUNK = 8
LANES = 128
GROUPS_PER_BLOCK = LANES // SSM_GROUP

US_TILE = 1024
SSM_ROWS = 256
MAIN_TILE = 256
CONV_HALO = 32
CONV_ROWS = 64
VMEM_LIMIT = 56 * 1024 * 1024


def _sigmoid(x):
    return 0.5 * jnp.tanh(0.5 * x) + 0.5


def _silu(x):
    h = 0.5 * x
    return h * jnp.tanh(h) + h


def _gelu_tanh(x):
    c = math.sqrt(2.0 / math.pi)
    return 0.5 * x * (1.0 + jnp.tanh(c * (x + 0.044715 * (x * x * x))))


def _rms_scale(x):
    return x * lax.rsqrt(jnp.mean(x * x, axis=-1, keepdims=True) + RMS_EPS)


def _const_spec(shape):
    zeros = (0,) * len(shape)
    return pl.BlockSpec(shape, lambda *_: zeros)


def _us_proj_kernel(x_ref, g_ref, w_ref, u_ref):
    h = _rms_scale(x_ref[...]) * g_ref[...]
    u_ref[...] = jnp.dot(h.astype(jnp.bfloat16), w_ref[...],
                         preferred_element_type=jnp.float32).astype(u_ref.dtype)


def _us_proj(x2, pre_g, w_us):
    t, d = x2.shape
    n = w_us.shape[1]
    return pl.pallas_call(
        _us_proj_kernel,
        grid=(t // US_TILE,),
        in_specs=[pl.BlockSpec((US_TILE, d), lambda i: (i, 0)),
                  _const_spec((1, d)),
                  _const_spec((d, n))],
        out_specs=pl.BlockSpec((US_TILE, n), lambda i: (i, 0)),
        out_shape=jax.ShapeDtypeStruct((t, n), jnp.bfloat16),
        compiler_params=pltpu.CompilerParams(dimension_semantics=("arbitrary",),
                                             vmem_limit_bytes=VMEM_LIMIT),
        name="us_proj",
    )(x2, pre_g, w_us)


def _ssm_weights(lam_re, lam_im, log_dt, b_re, b_im, c_re, c_im, d, n_levels):
    hp = lax.Precision.HIGHEST
    groups = lam_re.shape[0]
    nblk = groups // GROUPS_PER_BLOCK
    dt = jnp.exp(log_dt)[:, None]
    lr, li = lam_re, lam_im
    mag = jnp.exp(lr * dt)
    ar = mag * jnp.cos(li * dt)
    ai = mag * jnp.sin(li * dt)
    den = lr * lr + li * li
    zr = ((ar - 1.0) * lr + ai * li) / den
    zi = (ai * lr - (ar - 1.0) * li) / den
    bbr = zr[..., None] * b_re - zi[..., None] * b_im
    bbi = zr[..., None] * b_im + zi[..., None] * b_re
    pr, pi = [jnp.ones_like(ar)], [jnp.zeros_like(ai)]
    for _ in range(CHUNK):
        pr.append(pr[-1] * ar - pi[-1] * ai)
        pi.append(pr[-2] * ai + pi[-1] * ar)
    pr, pi = jnp.stack(pr), jnp.stack(pi)
    er = pr[:, :, :, None] * bbr - pi[:, :, :, None] * bbi
    ei = pr[:, :, :, None] * bbi + pi[:, :, :, None] * bbr
    kj = (jnp.einsum("gxp,jgpi->jgxi", c_re, er[:CHUNK], precision=hp)
          - jnp.einsum("gxp,jgpi->jgxi", c_im, ei[:CHUNK], precision=hp))
    kj = kj.at[0].add(d[:, None, :] * jnp.eye(SSM_GROUP, dtype=kj.dtype)[None])
    eye = jnp.eye(GROUPS_PER_BLOCK, dtype=jnp.float32)
    size = CHUNK * LANES

    lag = jnp.arange(CHUNK)[None, :] - jnp.arange(CHUNK)[:, None]
    kk = jnp.where((lag >= 0)[:, :, None, None, None], kj[jnp.clip(lag, 0, CHUNK - 1)], 0.0)
    kk = kk.reshape(CHUNK, CHUNK, nblk, GROUPS_PER_BLOCK, SSM_GROUP, SSM_GROUP)
    tz = jnp.einsum("ktbgxi,gq->bkgitqx", kk, eye).reshape(nblk, size, size)

    f = jnp.stack([er[CHUNK - 1::-1][:CHUNK], ei[CHUNK - 1::-1][:CHUNK]])
    f = f.reshape(2, CHUNK, nblk, GROUPS_PER_BLOCK, SSM_STATE, SSM_GROUP)
    wb = jnp.einsum("rkbgpi,gq->bkgirqp", f, eye).reshape(nblk, size, 2 * GROUPS_PER_BLOCK * SSM_STATE)

    gr = c_re[None] * pr[1:, :, None, :] - c_im[None] * pi[1:, :, None, :]
    gi = c_re[None] * pi[1:, :, None, :] + c_im[None] * pr[1:, :, None, :]
    m = jnp.stack([gr, -gi]).reshape(2, CHUNK, nblk, GROUPS_PER_BLOCK, SSM_GROUP, SSM_STATE)
    wc = jnp.einsum("rtbgxp,gq->brgptqx", m, eye).reshape(nblk, 2 * GROUPS_PER_BLOCK * SSM_STATE, size)

    lev_r, lev_i = [pr[CHUNK]], [pi[CHUNK]]
    for _ in range(n_levels - 1):
        lev_r.append(lev_r[-1] * lev_r[-1] - lev_i[-1] * lev_i[-1])
        lev_i.append(2.0 * lev_r[-2] * lev_i[-1])
    ap = jnp.stack([v for pair in zip(lev_r, lev_i) for v in pair])
    ap = ap.reshape(2 * n_levels, nblk, GROUPS_PER_BLOCK * SSM_STATE).transpose(1, 0, 2)
    return (tz.astype(jnp.bfloat16), wb.astype(jnp.bfloat16), wc.astype(jnp.bfloat16), ap)


def _ssm_chunk_kernel(u_ref, tz_ref, wb_ref, wc_ref, ap_ref, y_ref, cre_ref, cim_ref,
                      *, rows, n_levels, blocks_per_seq):
    half = GROUPS_PER_BLOCK * SSM_STATE

    @pl.when(pl.program_id(1) % blocks_per_seq == 0)
    def _():
        cre_ref[...] = jnp.zeros_like(cre_ref)
        cim_ref[...] = jnp.zeros_like(cim_ref)

    u = u_ref[0]
    y1 = jnp.dot(u, tz_ref[0], preferred_element_type=jnp.float32)
    bc = jnp.dot(u, wb_ref[0], preferred_element_type=jnp.float32)
    sre, sim = bc[:, :half], bc[:, half:]
    row = lax.broadcasted_iota(jnp.int32, (rows, 1), 0)
    first = row == 0
    cre, cim = cre_ref[...], cim_ref[...]
    a8r, a8i = ap_ref[0, 0:1, :], ap_ref[0, 1:2, :]
    sre = sre + jnp.where(first, a8r * cre - a8i * cim, 0.0)
    sim = sim + jnp.where(first, a8r * cim + a8i * cre, 0.0)
    for lev in range(n_levels):
        s = 1 << lev
        mr, mi = ap_ref[0, 2 * lev:2 * lev + 1, :], ap_ref[0, 2 * lev + 1:2 * lev + 2, :]
        keep = row >= s
        shr = jnp.where(keep, pltpu.roll(sre, s, 0), 0.0)
        shi = jnp.where(keep, pltpu.roll(sim, s, 0), 0.0)
        sre, sim = sre + (mr * shr - mi * shi), sim + (mr * shi + mi * shr)
    st_re = jnp.where(first, cre, pltpu.roll(sre, 1, 0))
    st_im = jnp.where(first, cim, pltpu.roll(sim, 1, 0))
    cre_ref[...] = sre[rows - 1:rows, :]
    cim_ref[...] = sim[rows - 1:rows, :]
    sst = jnp.concatenate([st_re, st_im], axis=1).astype(jnp.bfloat16)
    y2 = jnp.dot(sst, wc_ref[0], preferred_element_type=jnp.float32)
    y_ref[0] = (y1 + y2).astype(y_ref.dtype)


def _ssm_chunk(u4, tz, wb, wc, ap, rows_per_seq):
    nblk, r, size = u4.shape
    n_levels = ap.shape[1] // 2
    assert SSM_ROWS == 1 << n_levels and rows_per_seq % SSM_ROWS == 0
    kern = functools.partial(_ssm_chunk_kernel, rows=SSM_ROWS, n_levels=n_levels,
                             blocks_per_seq=rows_per_seq // SSM_ROWS)
    wspec = lambda a: pl.BlockSpec((1,) + a.shape[1:], lambda g, i: (g, 0, 0))
    return pl.pallas_call(
        kern,
        grid=(nblk, r // SSM_ROWS),
        in_specs=[pl.BlockSpec((1, SSM_ROWS, size), lambda g, i: (g, i, 0)),
                  wspec(tz), wspec(wb), wspec(wc), wspec(ap)],
        out_specs=pl.BlockSpec((1, SSM_ROWS, size), lambda g, i: (g, i, 0)),
        out_shape=jax.ShapeDtypeStruct((nblk, r, size), jnp.bfloat16),
        scratch_shapes=[pltpu.VMEM((1, GROUPS_PER_BLOCK * SSM_STATE), jnp.float32),
                        pltpu.VMEM((1, GROUPS_PER_BLOCK * SSM_STATE), jnp.float32)],
        compiler_params=pltpu.CompilerParams(dimension_semantics=("arbitrary", "arbitrary"),
                                             vmem_limit_bytes=VMEM_LIMIT),
        name="ssm_chunk",
    )(u4, tz, wb, wc, ap)


def _main_kernel(x_ref, y_ref, pre_g_ref, w_ref, cw_ref, cb_ref, lng_ref, lnb_ref, wco_ref,
                 wglu_ref, bglu_ref, wso_ref, wout_ref, post_g_ref, o_ref,
                 h_ref, cu_ref, cv_ref, *, tile, d, ds):
    bf16, f32 = jnp.bfloat16, jnp.float32
    o_cb, o_zc, o_zs, o_gc, o_gs = d, 2 * d, 3 * d, 3 * d + ds, 4 * d + ds

    @pl.when(pl.program_id(1) == 0)
    def _():
        cu_ref[0:CONV_HALO, :] = jnp.zeros((CONV_HALO, d), f32)

    x = x_ref[0]
    h_ref[...] = (_rms_scale(x) * pre_g_ref[...]).astype(bf16)

    def proj(lo, width):
        return jnp.dot(h_ref[...], w_ref[:, lo:lo + width], preferred_element_type=f32)

    cu_ref[CONV_HALO:CONV_HALO + tile, :] = proj(0, d) * _sigmoid(proj(o_cb, d))

    def conv_lane_block(j, carry):
        lanes = pl.ds(pl.multiple_of(j * LANES, LANES), LANES)
        taps = [jnp.broadcast_to(cw_ref[pl.ds(k, 1), lanes], (CONV_ROWS, LANES))
                for k in range(CONV_SIZE)]
        bias = jnp.broadcast_to(cb_ref[:, lanes], (CONV_ROWS, LANES))
        for r in range(tile // CONV_ROWS):
            acc = bias
            for lag in range(CONV_SIZE):
                acc = acc + taps[CONV_SIZE - 1 - lag] * cu_ref[
                    pl.ds(CONV_HALO + r * CONV_ROWS - lag, CONV_ROWS), lanes]
            cv_ref[pl.ds(r * CONV_ROWS, CONV_ROWS), lanes] = acc
        return carry

    lax.fori_loop(0, d // LANES, conv_lane_block, 0)
    cu_ref[0:CONV_HALO, :] = cu_ref[tile:tile + CONV_HALO, :]

    cv = cv_ref[...]
    mu = jnp.mean(cv, axis=-1, keepdims=True)
    xc = cv - mu
    var = jnp.mean(xc * xc, axis=-1, keepdims=True)
    ln = xc * lax.rsqrt(var + LN_EPS) * lng_ref[...] + lnb_ref[...]
    a_in = (_silu(ln) * _silu(proj(o_zc, d))).astype(bf16)
    conv_out = jnp.dot(a_in, wco_ref[...], preferred_element_type=f32)
    merged = _sigmoid(proj(o_gc, d)) * conv_out

    yg = _gelu_tanh(y_ref[0].astype(f32))
    lin = jnp.dot(yg.astype(bf16), wglu_ref[...], preferred_element_type=f32) + bglu_ref[...]
    b_in = (yg * _sigmoid(lin) * _silu(proj(o_zs, ds))).astype(bf16)
    ssm_out = jnp.dot(b_in, wso_ref[...], preferred_element_type=f32)
    merged = merged + _sigmoid(proj(o_gs, d)) * ssm_out

    out = jnp.dot(merged.astype(bf16), wout_ref[...], preferred_element_type=f32)
    o_ref[0] = x + _rms_scale(out) * post_g_ref[...]


def _main(x, y, pre_g, w_rest, cw, cb, lng, lnb, wco, wglu, bglu, wso, wout, post_g):
    b, l, d = x.shape
    ds = y.shape[-1]
    tile = MAIN_TILE
    kern = functools.partial(_main_kernel, tile=tile, d=d, ds=ds)
    consts = (pre_g, w_rest, cw, cb, lng, lnb, wco, wglu, bglu, wso, wout, post_g)
    return pl.pallas_call(
        kern,
        grid=(b, l // tile),
        in_specs=[pl.BlockSpec((1, tile, d), lambda i, j: (i, j, 0)),
                  pl.BlockSpec((1, tile, ds), lambda i, j: (i, j, 0))]
                 + [_const_spec(a.shape) for a in consts],
        out_specs=pl.BlockSpec((1, tile, d), lambda i, j: (i, j, 0)),
        out_shape=jax.ShapeDtypeStruct((b, l, d), x.dtype),
        scratch_shapes=[pltpu.VMEM((tile, d), jnp.bfloat16),
                        pltpu.VMEM((CONV_HALO + tile, d), jnp.float32),
                        pltpu.VMEM((tile, d), jnp.float32)],
        compiler_params=pltpu.CompilerParams(dimension_semantics=("arbitrary", "arbitrary"),
                                             vmem_limit_bytes=VMEM_LIMIT),
        name="main_block",
    )(x, y, *consts)


def _layer(x, pre_g, w_in, conv_w, conv_b, ln_g, ln_b, w_conv_out, lam_re, lam_im, log_dt,
           b_re, b_im, c_re, c_im, d_skip, w_glu, b_glu, w_ssm_out, w_out, post_g):
    bsz, length, d = x.shape
    ds = w_glu.shape[0]
    groups = ds // SSM_GROUP
    nblk = ds // LANES
    bf16 = jnp.bfloat16
    o_us = 3 * d
    w_us = w_in[:, o_us:o_us + ds].astype(bf16)
    w_rest = jnp.concatenate([w_in[:, :o_us], w_in[:, o_us + ds:]], axis=1).astype(bf16)
    row = lambda v: v.reshape(1, -1)

    t = bsz * length
    u = _us_proj(x.reshape(t, d), row(pre_g), w_us)
    rows = t // CHUNK
    u4 = u.reshape(rows, CHUNK, nblk, LANES).transpose(2, 0, 1, 3).reshape(nblk, rows, CHUNK * LANES)
    n_levels = SSM_ROWS.bit_length() - 1
    tz, wb, wc, ap = _ssm_weights(lam_re, lam_im, log_dt, b_re, b_im, c_re, c_im, d_skip, n_levels)
    assert lam_re.shape[0] == groups
    y4 = _ssm_chunk(u4, tz, wb, wc, ap, length // CHUNK)
    y = y4.reshape(nblk, rows, CHUNK, LANES).transpose(1, 2, 0, 3).reshape(bsz, length, ds)

    cw = jnp.concatenate([conv_w, jnp.zeros((CONV_HALO - CONV_SIZE, d), conv_w.dtype)], axis=0)
    return _main(x, y, row(pre_g), w_rest, cw, row(conv_b), row(ln_g), row(ln_b),
                 w_conv_out.astype(bf16), w_glu.astype(bf16), row(b_glu), w_ssm_out.astype(bf16),
                 w_out.astype(bf16), row(post_g))


def kernel(x, pre_norm_gain, w_in, conv_w, conv_b, conv_ln_gain, conv_ln_bias, w_conv_out, ssm_lambda_re, ssm_lambda_im, ssm_log_dt, ssm_b_re, ssm_b_im, ssm_c_re, ssm_c_im, ssm_d, w_ssm_glu, b_ssm_glu, w_ssm_out, w_out, post_norm_gain):
    for l in range(pre_norm_gain.shape[0]):
        x = _layer(x, pre_norm_gain[l], w_in[l], conv_w[l], conv_b[l], conv_ln_gain[l],
                   conv_ln_bias[l], w_conv_out[l], ssm_lambda_re[l], ssm_lambda_im[l],
                   ssm_log_dt[l], ssm_b_re[l], ssm_b_im[l], ssm_c_re[l], ssm_c_im[l], ssm_d[l],
                   w_ssm_glu[l], b_ssm_glu[l], w_ssm_out[l], w_out[l], post_norm_gain[l])
    return x
```

```python
import functools
import math

import jax
import jax.numpy as jnp
from jax import lax
from jax.experimental import pallas as pl
from jax.experimental.pallas import tpu as pltpu

RMS_EPS = 1e-6
LN_EPS = 1e-5
CONV_SIZE = 31
SSM_GROUP = 16
SSM_STATE = 64
CHUNK = 8
LANES = 128
SUBLANES = 8
GROUPS_PER_BLOCK = LANES // SSM_GROUP
BLOCK_STATES = GROUPS_PER_BLOCK * SSM_STATE
ROW_WIDTH = CHUNK * LANES

US_TILE = 1024
SSM_ROWS = 256
MAIN_TILE = 256
CONV_HALO = 32
CONV_ROWS = 64
MXU_COLS = 256
EW_ROWS = 16
VMEM_LIMIT = 56 * 1024 * 1024


def _sigmoid(x):
    return 0.5 * jnp.tanh(0.5 * x) + 0.5


def _silu(x):
    h = 0.5 * x
    return h * jnp.tanh(h) + h


def _gelu_tanh(x):
    c = math.sqrt(2.0 / math.pi)
    return 0.5 * x * (1.0 + jnp.tanh(c * (x + 0.044715 * (x * x * x))))


def _rms_scale(x):
    return x * lax.rsqrt(jnp.mean(x * x, axis=-1, keepdims=True) + RMS_EPS)


def _const_spec(shape):
    zeros = (0,) * len(shape)
    return pl.BlockSpec(shape, lambda *_: zeros)


def _us_proj_kernel(x_ref, g_ref, w_ref, u_ref, slab_ref, *, tile, nblk):
    h = _rms_scale(x_ref[...]) * g_ref[...]
    u = jnp.dot(h.astype(jnp.bfloat16), w_ref[...], preferred_element_type=jnp.float32)
    for b in range(nblk):
        slab_ref[b] = u[:, b * LANES:(b + 1) * LANES]
    for b in range(nblk):
        for k in range(CHUNK):
            u_ref[b, :, k * LANES:(k + 1) * LANES] = slab_ref[
                b, pl.ds(k, tile // CHUNK, stride=CHUNK), :].astype(u_ref.dtype)


def _us_proj(x2, pre_g, w_us):
    t, d = x2.shape
    n = w_us.shape[1]
    nblk = n // LANES
    kern = functools.partial(_us_proj_kernel, tile=US_TILE, nblk=nblk)
    return pl.pallas_call(
        kern,
        grid=(t // US_TILE,),
        in_specs=[pl.BlockSpec((US_TILE, d), lambda i: (i, 0)),
                  _const_spec((1, d)),
                  _const_spec((d, n))],
        out_specs=pl.BlockSpec((nblk, US_TILE // CHUNK, ROW_WIDTH), lambda i: (0, i, 0)),
        out_shape=jax.ShapeDtypeStruct((nblk, t // CHUNK, ROW_WIDTH), jnp.bfloat16),
        scratch_shapes=[pltpu.VMEM((nblk, US_TILE, LANES), jnp.float32)],
        compiler_params=pltpu.CompilerParams(dimension_semantics=("arbitrary",),
                                             vmem_limit_bytes=VMEM_LIMIT),
        name="us_proj",
    )(x2, pre_g, w_us)


def _ssm_compact(lam_re, lam_im, log_dt, b_re, b_im, c_re, c_im, d, n_levels):
    groups, states = lam_re.shape
    gh = groups * SSM_GROUP
    dt = jnp.exp(log_dt)[:, None]
    lr, li = lam_re, lam_im
    mag = jnp.exp(lr * dt)
    ar = mag * jnp.cos(li * dt)
    ai = mag * jnp.sin(li * dt)
    den = lr * lr + li * li
    zr = ((ar - 1.0) * lr + ai * li) / den
    zi = (ai * lr - (ar - 1.0) * li) / den
    bbr = zr[..., None] * b_re - zi[..., None] * b_im
    bbi = zr[..., None] * b_im + zi[..., None] * b_re
    pr, pi = [jnp.ones_like(ar)], [jnp.zeros_like(ai)]
    for _ in range(CHUNK):
        pr.append(pr[-1] * ar - pi[-1] * ai)
        pi.append(pr[-2] * ai + pi[-1] * ar)
    pr, pi = jnp.stack(pr), jnp.stack(pi)

    bt_r = bbr.transpose(2, 0, 1).reshape(1, SSM_GROUP, groups * states)
    bt_i = bbi.transpose(2, 0, 1).reshape(1, SSM_GROUP, groups * states)
    qr = pr[CHUNK - 1::-1].reshape(CHUNK, 1, groups * states)
    qi = pi[CHUNK - 1::-1].reshape(CHUNK, 1, groups * states)
    fb = jnp.stack([qr * bt_r - qi * bt_i, qr * bt_i + qi * bt_r])

    spread = lambda v: jnp.repeat(jnp.swapaxes(v, -1, -2), SSM_GROUP, axis=-1)
    px_r, px_i = spread(pr), spread(pi)
    ct_r = c_re.transpose(2, 0, 1).reshape(1, states, gh)
    ct_i = c_im.transpose(2, 0, 1).reshape(1, states, gh)
    m_r = ct_r * px_r - ct_i * px_i
    m_i = ct_r * px_i + ct_i * px_r
    mb = jnp.stack([m_r[1:], -m_i[1:]])
    bx_r = spread(bbr.transpose(2, 0, 1))
    bx_i = spread(bbi.transpose(2, 0, 1))
    kb = jnp.sum(m_r[:CHUNK, None] * bx_r[None] - m_i[:CHUNK, None] * bx_i[None], axis=2)
    skip = (d.T[:, :, None] * jnp.eye(SSM_GROUP, dtype=d.dtype)[:, None, :]).reshape(SSM_GROUP, gh)
    kb = kb.at[0].add(skip)

    lev_r, lev_i = [pr[CHUNK]], [pi[CHUNK]]
    for _ in range(n_levels - 1):
        lev_r.append(lev_r[-1] * lev_r[-1] - lev_i[-1] * lev_i[-1])
        lev_i.append(2.0 * lev_r[-2] * lev_i[-1])
    ap = jnp.stack([v for pair in zip(lev_r, lev_i) for v in pair])
    ap = ap.reshape(2 * n_levels, groups // GROUPS_PER_BLOCK, BLOCK_STATES).transpose(1, 0, 2)
    return kb, fb, mb, ap


def _block_diag(compact, group_rows, group_cols):
    n = compact.shape[1]
    tiled = jnp.concatenate([compact] * GROUPS_PER_BLOCK, axis=0)
    r = lax.broadcasted_iota(jnp.int32, tiled.shape, 0) // group_rows
    c = lax.broadcasted_iota(jnp.int32, tiled.shape, 1) // group_cols
    return jnp.where(r == c, tiled, 0.0).astype(jnp.bfloat16)


def _ssm_chunk_kernel(u_ref, kb_ref, fb_ref, mb_ref, ap_ref, y_ref,
                      tz_ref, wb_ref, wc_ref, cre_ref, cim_ref, *, rows, n_levels, blocks_per_seq):
    half = BLOCK_STATES

    @pl.when(pl.program_id(1) == 0)
    def _():
        tz_ref[...] = jnp.zeros_like(tz_ref)
        for lag in range(CHUNK):
            blk = _block_diag(kb_ref[lag], SSM_GROUP, SSM_GROUP)
            for k in range(CHUNK - lag):
                t = k + lag
                tz_ref[k * LANES:(k + 1) * LANES, t * LANES:(t + 1) * LANES] = blk
        for ri in range(2):
            for k in range(CHUNK):
                wb_ref[k * LANES:(k + 1) * LANES, ri * half:(ri + 1) * half] = _block_diag(
                    fb_ref[ri, k], SSM_GROUP, SSM_STATE)
                wc_ref[ri * half:(ri + 1) * half, k * LANES:(k + 1) * LANES] = _block_diag(
                    mb_ref[ri, k], SSM_STATE, SSM_GROUP)

    @pl.when(pl.program_id(1) % blocks_per_seq == 0)
    def _():
        cre_ref[...] = jnp.zeros_like(cre_ref)
        cim_ref[...] = jnp.zeros_like(cim_ref)

    u = u_ref[0]
    y1 = jnp.dot(u, tz_ref[...], preferred_element_type=jnp.float32)
    bc = jnp.dot(u, wb_ref[...], preferred_element_type=jnp.float32)
    sre, sim = bc[:, :half], bc[:, half:]
    row = lax.broadcasted_iota(jnp.int32, (rows, 1), 0)
    first = row == 0
    cre, cim = cre_ref[...], cim_ref[...]
    a8r, a8i = ap_ref[0, 0:1, :], ap_ref[0, 1:2, :]
    sre = sre + jnp.where(first, a8r * cre - a8i * cim, 0.0)
    sim = sim + jnp.where(first, a8r * cim + a8i * cre, 0.0)
    for lev in range(n_levels):
        s = 1 << lev
        mr, mi = ap_ref[0, 2 * lev:2 * lev + 1, :], ap_ref[0, 2 * lev + 1:2 * lev + 2, :]
        keep = row >= s
        shr = jnp.where(keep, pltpu.roll(sre, s, 0), 0.0)
        shi = jnp.where(keep, pltpu.roll(sim, s, 0), 0.0)
        sre, sim = sre + (mr * shr - mi * shi), sim + (mr * shi + mi * shr)
    st_re = jnp.where(first, cre, pltpu.roll(sre, 1, 0))
    st_im = jnp.where(first, cim, pltpu.roll(sim, 1, 0))
    cre_ref[...] = sre[rows - 1:rows, :]
    cim_ref[...] = sim[rows - 1:rows, :]
    sst = jnp.concatenate([st_re, st_im], axis=1).astype(jnp.bfloat16)
    y = y1 + jnp.dot(sst, wc_ref[...], preferred_element_type=jnp.float32)
    for t in range(CHUNK):
        y_ref[pl.ds(t, rows, stride=CHUNK), :] = y[:, t * LANES:(t + 1) * LANES]


def _ssm_chunk(u4, kb, fb, mb, ap, rows_per_seq):
    nblk, r, size = u4.shape
    n_levels = ap.shape[1] // 2
    assert SSM_ROWS == 1 << n_levels and rows_per_seq % SSM_ROWS == 0 and size == ROW_WIDTH
    kern = functools.partial(_ssm_chunk_kernel, rows=SSM_ROWS, n_levels=n_levels,
                             blocks_per_seq=rows_per_seq // SSM_ROWS)
    mat = pltpu.VMEM((ROW_WIDTH, ROW_WIDTH), jnp.bfloat16)
    assert 2 * BLOCK_STATES == ROW_WIDTH
    return pl.pallas_call(
        kern,
        grid=(nblk, r // SSM_ROWS),
        in_specs=[pl.BlockSpec((1, SSM_ROWS, size), lambda g, i: (g, i, 0)),
                  pl.BlockSpec((CHUNK, SSM_GROUP, LANES), lambda g, i: (0, 0, g)),
                  pl.BlockSpec((2, CHUNK, SSM_GROUP, BLOCK_STATES), lambda g, i: (0, 0, 0, g)),
                  pl.BlockSpec((2, CHUNK, SSM_STATE, LANES), lambda g, i: (0, 0, 0, g)),
                  pl.BlockSpec((1,) + ap.shape[1:], lambda g, i: (g, 0, 0))],
        out_specs=pl.BlockSpec((SSM_ROWS * CHUNK, LANES), lambda g, i: (i, g)),
        out_shape=jax.ShapeDtypeStruct((r * CHUNK, nblk * LANES), jnp.float32),
        scratch_shapes=[mat, mat, mat,
                        pltpu.VMEM((1, BLOCK_STATES), jnp.float32),
                        pltpu.VMEM((1, BLOCK_STATES), jnp.float32)],
        compiler_params=pltpu.CompilerParams(dimension_semantics=("arbitrary", "arbitrary"),
                                             vmem_limit_bytes=VMEM_LIMIT),
        name="ssm_chunk",
    )(u4, kb, fb, mb, ap)


def _main_kernel(x_ref, y_ref, pre_g_ref, w_ref, cw_ref, cb_ref, lng_ref, lnb_ref, wco_ref,
                 wglu_ref, bglu_ref, wso_ref, wout_ref, post_g_ref, o_ref,
                 h_ref, glu_ref, cs_ref, cv_ref, pz_ref, so_ref, a_ref, acc_ref, yg_ref, yb_ref,
                 lin_ref, *, tile, d, ds):
    bf16, f32 = jnp.bfloat16, jnp.float32
    o_zc = 2 * d
    p_zs, p_gc, p_gs, p_end = d, d + ds, 2 * d + ds, 3 * d + ds
    nlane = d // LANES
    pieces = [slice(r, r + EW_ROWS) for r in range(0, tile, EW_ROWS)]

    @pl.when(pl.program_id(1) == 0)
    def _():
        cs_ref[:, 0:CONV_HALO, :] = jnp.zeros((nlane, CONV_HALO, LANES), f32)

    for rows in pieces:
        h_ref[rows, :] = (_rms_scale(x_ref[0, rows, :]) * pre_g_ref[...]).astype(bf16)

    def proj(dst_ref, dst_col, src_col):
        dst_ref[:, dst_col:dst_col + MXU_COLS] = jnp.dot(
            h_ref[...], w_ref[:, src_col:src_col + MXU_COLS], preferred_element_type=f32)

    for c in range(0, 2 * d, MXU_COLS):
        proj(glu_ref, c, c)
    for rows in pieces:
        cu = glu_ref[rows, 0:d] * _sigmoid(glu_ref[rows, d:2 * d])
        for j in range(nlane):
            cs_ref[j, CONV_HALO + rows.start:CONV_HALO + rows.stop, :] = cu[:, j * LANES:(j + 1) * LANES]

    def ssm_glu():
        for rows in pieces:
            yg = _gelu_tanh(y_ref[0, rows, :])
            yg_ref[rows, :] = yg
            yb_ref[rows, :] = yg.astype(bf16)
        lin_ref[...] = jnp.dot(yb_ref[...], wglu_ref[...], preferred_element_type=f32)

    def ssm_out():
        for rows in pieces:
            gate = _sigmoid(lin_ref[rows, :] + bglu_ref[...]) * _silu(pz_ref[rows, p_zs:p_gc])
            yb_ref[rows, :] = (yg_ref[rows, :] * gate).astype(bf16)
        so_ref[...] = jnp.dot(yb_ref[...], wso_ref[...], preferred_element_type=f32)

    def conv_lane_block(j, lanes):
        for r in range(tile // CONV_ROWS):
            acc = jnp.broadcast_to(cb_ref[:, lanes], (CONV_ROWS, LANES))
            for lag in range(CONV_SIZE):
                lo = CONV_HALO + r * CONV_ROWS - lag
                tap = cw_ref[CONV_SIZE - 1 - lag:CONV_SIZE - lag, lanes]
                acc = acc + tap * cs_ref[j, lo:lo + CONV_ROWS, :]
            cv_ref[r * CONV_ROWS:(r + 1) * CONV_ROWS, lanes] = acc

    side_chunks = p_end // MXU_COLS
    per_step = side_chunks // (nlane - 1)
    assert per_step * (nlane - 1) == side_chunks

    def conv_and_proj(j, carry):
        conv_lane_block(j, pl.ds(pl.multiple_of(j * LANES, LANES), LANES))
        for q in range(per_step):
            col = pl.multiple_of((j * per_step + q) * MXU_COLS, MXU_COLS)
            pz_ref[:, pl.ds(col, MXU_COLS)] = jnp.dot(
                h_ref[...], w_ref[:, pl.ds(o_zc + col, MXU_COLS)], preferred_element_type=f32)
        return carry

    lax.fori_loop(0, nlane - 1, conv_and_proj, 0)
    conv_lane_block(nlane - 1, slice((nlane - 1) * LANES, nlane * LANES))
    ssm_glu()
    ssm_out()
    cs_ref[:, 0:CONV_HALO, :] = cs_ref[:, tile:CONV_HALO + tile, :]

    for rows in pieces:
        cv = cv_ref[rows, :]
        xc = cv - jnp.mean(cv, axis=-1, keepdims=True)
        var = jnp.mean(xc * xc, axis=-1, keepdims=True)
        ln = xc * lax.rsqrt(var + LN_EPS) * lng_ref[...] + lnb_ref[...]
        a_ref[rows, :] = (_silu(ln) * _silu(pz_ref[rows, 0:p_zs])).astype(bf16)
    acc_ref[...] = jnp.dot(a_ref[...], wco_ref[...], preferred_element_type=f32)

    for rows in pieces:
        merged = (_sigmoid(pz_ref[rows, p_gc:p_gs]) * acc_ref[rows, :]
                  + _sigmoid(pz_ref[rows, p_gs:p_end]) * so_ref[rows, :])
        a_ref[rows, :] = merged.astype(bf16)
    acc_ref[...] = jnp.dot(a_ref[...], wout_ref[...], preferred_element_type=f32)
    for rows in pieces:
        o_ref[0, rows, :] = x_ref[0, rows, :] + _rms_scale(acc_ref[rows, :]) * post_g_ref[...]


def _main(x, y, pre_g, w_rest, cw, cb, lng, lnb, wco, wglu, bglu, wso, wout, post_g):
    b, l, d = x.shape
    ds = y.shape[-1]
    tile = MAIN_TILE
    assert CONV_HALO >= CONV_SIZE - 1 and CONV_HALO % SUBLANES == 0 and tile >= CONV_HALO
    kern = functools.partial(_main_kernel, tile=tile, d=d, ds=ds)
    consts = (pre_g, w_rest, cw, cb, lng, lnb, wco, wglu, bglu, wso, wout, post_g)
    return pl.pallas_call(
        kern,
        grid=(b, l // tile),
        in_specs=[pl.BlockSpec((1, tile, d), lambda i, j: (i, j, 0)),
                  pl.BlockSpec((1, tile, ds), lambda i, j: (i, j, 0))]
                 + [_const_spec(a.shape) for a in consts],
        out_specs=pl.BlockSpec((1, tile, d), lambda i, j: (i, j, 0)),
        out_shape=jax.ShapeDtypeStruct((b, l, d), x.dtype),
        scratch_shapes=[pltpu.VMEM((tile, d), jnp.bfloat16),
                        pltpu.VMEM((tile, 2 * d), jnp.float32),
                        pltpu.VMEM((d // LANES, CONV_HALO + tile, LANES), jnp.float32),
                        pltpu.VMEM((tile, d), jnp.float32),
                        pltpu.VMEM((tile, 3 * d + ds), jnp.float32),
                        pltpu.VMEM((tile, d), jnp.float32),
                        pltpu.VMEM((tile, d), jnp.bfloat16),
                        pltpu.VMEM((tile, d), jnp.float32),
                        pltpu.VMEM((tile, ds), jnp.float32),
                        pltpu.VMEM((tile, ds), jnp.bfloat16),
                        pltpu.VMEM((tile, ds), jnp.float32)],
        compiler_params=pltpu.CompilerParams(dimension_semantics=("arbitrary", "arbitrary"),
                                             vmem_limit_bytes=VMEM_LIMIT),
        name="main_block",
    )(x, y, *consts)


def _layer(x, pre_g, w_in, conv_w, conv_b, ln_g, ln_b, w_conv_out, lam_re, lam_im, log_dt,
           b_re, b_im, c_re, c_im, d_skip, w_glu, b_glu, w_ssm_out, w_out, post_g):
    bsz, length, d = x.shape
    ds = w_glu.shape[0]
    bf16 = jnp.bfloat16
    o_us = 3 * d
    w_us = w_in[:, o_us:o_us + ds].astype(bf16)
    w_rest = jnp.concatenate([w_in[:, :o_us], w_in[:, o_us + ds:]], axis=1).astype(bf16)
    row = lambda v: v.reshape(1, -1)
    assert lam_re.shape == (ds // SSM_GROUP, SSM_STATE)

    u4 = _us_proj(x.reshape(bsz * length, d), row(pre_g), w_us)
    n_levels = SSM_ROWS.bit_length() - 1
    kb, fb, mb, ap = _ssm_compact(lam_re, lam_im, log_dt, b_re, b_im, c_re, c_im, d_skip, n_levels)
    y = _ssm_chunk(u4, kb, fb, mb, ap, length // CHUNK).reshape(bsz, length, ds)

    cw = jnp.concatenate([conv_w, jnp.zeros((SUBLANES - CONV_SIZE % SUBLANES, d), conv_w.dtype)], axis=0)
    return _main(x, y, row(pre_g), w_rest, cw, row(conv_b), row(ln_g), row(ln_b),
                 w_conv_out.astype(bf16), w_glu.astype(bf16), row(b_glu), w_ssm_out.astype(bf16),
                 w_out.astype(bf16), row(post_g))


def kernel(x, pre_norm_gain, w_in, conv_w, conv_b, conv_ln_gain, conv_ln_bias, w_conv_out, ssm_lambda_re, ssm_lambda_im, ssm_log_dt, ssm_b_re, ssm_b_im, ssm_c_re, ssm_c_im, ssm_d, w_ssm_glu, b_ssm_glu, w_ssm_out, w_out, post_norm_gain):
    for l in range(pre_norm_gain.shape[0]):
        x = _layer(x, pre_norm_gain[l], w_in[l], conv_w[l], conv_b[l], conv_ln_gain[l],
                   conv_ln_bias[l], w_conv_out[l], ssm_lambda_re[l], ssm_lambda_im[l],
                   ssm_log_dt[l], ssm_b_re[l], ssm_b_im[l], ssm_c_re[l], ssm_c_im[l], ssm_d[l],
                   w_ssm_glu[l], b_ssm_glu[l], w_ssm_out[l], w_out[l], post_norm_gain[l])
    return x
```

```python
import functools
import math

import jax
import jax.numpy as jnp
from jax import lax
from jax.experimental import pallas as pl
from jax.experimental.pallas import tpu as pltpu

RMS_EPS = 1e-6
LN_EPS = 1e-5
CONV_SIZE = 31
SSM_GROUP = 16
SSM_STATE = 64
CHUNK = 8
LANES = 128
SUBLANES = 8
GROUPS_PER_BLOCK = LANES // SSM_GROUP
BLOCK_STATES = GROUPS_PER_BLOCK * SSM_STATE
ROW_WIDTH = CHUNK * LANES

US_TILE = 1024
SSM_ROWS = 256
MAIN_TILE = 512
CONV_HALO = 32
CONV_ROWS = 64
CONV_BLOCKS = 1
MXU_COLS = 256
EW_ROWS = 16
VMEM_LIMIT = 56 * 1024 * 1024


def _sigmoid(x):
    return 0.5 * jnp.tanh(0.5 * x) + 0.5


def _silu(x):
    h = 0.5 * x
    return h * jnp.tanh(h) + h


def _gelu_tanh(x):
    c = math.sqrt(2.0 / math.pi)
    return 0.5 * x * (1.0 + jnp.tanh(c * (x + 0.044715 * (x * x * x))))


def _rms_scale(x):
    return x * lax.rsqrt(jnp.mean(x * x, axis=-1, keepdims=True) + RMS_EPS)


def _const_spec(shape):
    zeros = (0,) * len(shape)
    return pl.BlockSpec(shape, lambda *_: zeros)


def _us_proj_kernel(x_ref, g_ref, w_ref, u_ref, slab_ref, *, tile, nblk):
    h = _rms_scale(x_ref[...]) * g_ref[...]
    u = jnp.dot(h.astype(jnp.bfloat16), w_ref[...], preferred_element_type=jnp.float32)
    for b in range(nblk):
        slab_ref[b] = u[:, b * LANES:(b + 1) * LANES]
    for b in range(nblk):
        for k in range(CHUNK):
            u_ref[b, :, k * LANES:(k + 1) * LANES] = slab_ref[
                b, pl.ds(k, tile // CHUNK, stride=CHUNK), :].astype(u_ref.dtype)


def _us_proj(x2, pre_g, w_us):
    t, d = x2.shape
    n = w_us.shape[1]
    nblk = n // LANES
    kern = functools.partial(_us_proj_kernel, tile=US_TILE, nblk=nblk)
    return pl.pallas_call(
        kern,
        grid=(t // US_TILE,),
        in_specs=[pl.BlockSpec((US_TILE, d), lambda i: (i, 0)),
                  _const_spec((1, d)),
                  _const_spec((d, n))],
        out_specs=pl.BlockSpec((nblk, US_TILE // CHUNK, ROW_WIDTH), lambda i: (0, i, 0)),
        out_shape=jax.ShapeDtypeStruct((nblk, t // CHUNK, ROW_WIDTH), jnp.bfloat16),
        scratch_shapes=[pltpu.VMEM((nblk, US_TILE, LANES), jnp.float32)],
        compiler_params=pltpu.CompilerParams(dimension_semantics=("arbitrary",),
                                             vmem_limit_bytes=VMEM_LIMIT),
        name="us_proj",
    )(x2, pre_g, w_us)


def _ssm_compact(lam_re, lam_im, log_dt, b_re, b_im, c_re, c_im, d, n_levels):
    groups, states = lam_re.shape
    gh = groups * SSM_GROUP
    dt = jnp.exp(log_dt)[:, None]
    lr, li = lam_re, lam_im
    mag = jnp.exp(lr * dt)
    ar = mag * jnp.cos(li * dt)
    ai = mag * jnp.sin(li * dt)
    den = lr * lr + li * li
    zr = ((ar - 1.0) * lr + ai * li) / den
    zi = (ai * lr - (ar - 1.0) * li) / den
    bbr = zr[..., None] * b_re - zi[..., None] * b_im
    bbi = zr[..., None] * b_im + zi[..., None] * b_re
    pr, pi = [jnp.ones_like(ar)], [jnp.zeros_like(ai)]
    for _ in range(CHUNK):
        pr.append(pr[-1] * ar - pi[-1] * ai)
        pi.append(pr[-2] * ai + pi[-1] * ar)
    pr, pi = jnp.stack(pr), jnp.stack(pi)

    bt_r = bbr.transpose(2, 0, 1).reshape(1, SSM_GROUP, groups * states)
    bt_i = bbi.transpose(2, 0, 1).reshape(1, SSM_GROUP, groups * states)
    qr = pr[CHUNK - 1::-1].reshape(CHUNK, 1, groups * states)
    qi = pi[CHUNK - 1::-1].reshape(CHUNK, 1, groups * states)
    fb = jnp.stack([qr * bt_r - qi * bt_i, qr * bt_i + qi * bt_r])

    spread = lambda v: jnp.repeat(jnp.swapaxes(v, -1, -2), SSM_GROUP, axis=-1)
    px_r, px_i = spread(pr), spread(pi)
    ct_r = c_re.transpose(2, 0, 1).reshape(1, states, gh)
    ct_i = c_im.transpose(2, 0, 1).reshape(1, states, gh)
    m_r = ct_r * px_r - ct_i * px_i
    m_i = ct_r * px_i + ct_i * px_r
    mb = jnp.stack([m_r[1:], -m_i[1:]])
    bx_r = spread(bbr.transpose(2, 0, 1))
    bx_i = spread(bbi.transpose(2, 0, 1))
    kb = jnp.sum(m_r[:CHUNK, None] * bx_r[None] - m_i[:CHUNK, None] * bx_i[None], axis=2)
    skip = (d.T[:, :, None] * jnp.eye(SSM_GROUP, dtype=d.dtype)[:, None, :]).reshape(SSM_GROUP, gh)
    kb = kb.at[0].add(skip)

    lev_r, lev_i = [pr[CHUNK]], [pi[CHUNK]]
    for _ in range(n_levels - 1):
        lev_r.append(lev_r[-1] * lev_r[-1] - lev_i[-1] * lev_i[-1])
        lev_i.append(2.0 * lev_r[-2] * lev_i[-1])
    ap = jnp.stack([v for pair in zip(lev_r, lev_i) for v in pair])
    ap = ap.reshape(2 * n_levels, groups // GROUPS_PER_BLOCK, BLOCK_STATES).transpose(1, 0, 2)
    return kb, fb, mb, ap


def _block_diag(compact, group_rows, group_cols):
    n = compact.shape[1]
    tiled = jnp.concatenate([compact] * GROUPS_PER_BLOCK, axis=0)
    r = lax.broadcasted_iota(jnp.int32, tiled.shape, 0) // group_rows
    c = lax.broadcasted_iota(jnp.int32, tiled.shape, 1) // group_cols
    return jnp.where(r == c, tiled, 0.0).astype(jnp.bfloat16)


def _ssm_chunk_kernel(u_ref, kb_ref, fb_ref, mb_ref, ap_ref, y_ref,
                      tz_ref, wb_ref, wc_ref, cre_ref, cim_ref, *, rows, n_levels, blocks_per_seq):
    half = BLOCK_STATES

    @pl.when(pl.program_id(1) == 0)
    def _():
        tz_ref[...] = jnp.zeros_like(tz_ref)
        for lag in range(CHUNK):
            blk = _block_diag(kb_ref[lag], SSM_GROUP, SSM_GROUP)
            for k in range(CHUNK - lag):
                t = k + lag
                tz_ref[k * LANES:(k + 1) * LANES, t * LANES:(t + 1) * LANES] = blk
        for ri in range(2):
            for k in range(CHUNK):
                wb_ref[k * LANES:(k + 1) * LANES, ri * half:(ri + 1) * half] = _block_diag(
                    fb_ref[ri, k], SSM_GROUP, SSM_STATE)
                wc_ref[ri * half:(ri + 1) * half, k * LANES:(k + 1) * LANES] = _block_diag(
                    mb_ref[ri, k], SSM_STATE, SSM_GROUP)

    @pl.when(pl.program_id(1) % blocks_per_seq == 0)
    def _():
        cre_ref[...] = jnp.zeros_like(cre_ref)
        cim_ref[...] = jnp.zeros_like(cim_ref)

    u = u_ref[0]
    y1 = jnp.dot(u, tz_ref[...], preferred_element_type=jnp.float32)
    bc = jnp.dot(u, wb_ref[...], preferred_element_type=jnp.float32)
    sre, sim = bc[:, :half], bc[:, half:]
    row = lax.broadcasted_iota(jnp.int32, (rows, 1), 0)
    first = row == 0
    cre, cim = cre_ref[...], cim_ref[...]
    a8r, a8i = ap_ref[0, 0:1, :], ap_ref[0, 1:2, :]
    sre = sre + jnp.where(first, a8r * cre - a8i * cim, 0.0)
    sim = sim + jnp.where(first, a8r * cim + a8i * cre, 0.0)
    for lev in range(n_levels):
        s = 1 << lev
        mr, mi = ap_ref[0, 2 * lev:2 * lev + 1, :], ap_ref[0, 2 * lev + 1:2 * lev + 2, :]
        keep = row >= s
        shr = jnp.where(keep, pltpu.roll(sre, s, 0), 0.0)
        shi = jnp.where(keep, pltpu.roll(sim, s, 0), 0.0)
        sre, sim = sre + (mr * shr - mi * shi), sim + (mr * shi + mi * shr)
    st_re = jnp.where(first, cre, pltpu.roll(sre, 1, 0))
    st_im = jnp.where(first, cim, pltpu.roll(sim, 1, 0))
    cre_ref[...] = sre[rows - 1:rows, :]
    cim_ref[...] = sim[rows - 1:rows, :]
    sst = jnp.concatenate([st_re, st_im], axis=1).astype(jnp.bfloat16)
    y = y1 + jnp.dot(sst, wc_ref[...], preferred_element_type=jnp.float32)
    for t in range(CHUNK):
        y_ref[pl.ds(t, rows, stride=CHUNK), :] = y[:, t * LANES:(t + 1) * LANES]


def _ssm_chunk(u4, kb, fb, mb, ap, rows_per_seq):
    nblk, r, size = u4.shape
    n_levels = ap.shape[1] // 2
    assert SSM_ROWS == 1 << n_levels and rows_per_seq % SSM_ROWS == 0 and size == ROW_WIDTH
    kern = functools.partial(_ssm_chunk_kernel, rows=SSM_ROWS, n_levels=n_levels,
                             blocks_per_seq=rows_per_seq // SSM_ROWS)
    mat = pltpu.VMEM((ROW_WIDTH, ROW_WIDTH), jnp.bfloat16)
    assert 2 * BLOCK_STATES == ROW_WIDTH
    return pl.pallas_call(
        kern,
        grid=(nblk, r // SSM_ROWS),
        in_specs=[pl.BlockSpec((1, SSM_ROWS, size), lambda g, i: (g, i, 0)),
                  pl.BlockSpec((CHUNK, SSM_GROUP, LANES), lambda g, i: (0, 0, g)),
                  pl.BlockSpec((2, CHUNK, SSM_GROUP, BLOCK_STATES), lambda g, i: (0, 0, 0, g)),
                  pl.BlockSpec((2, CHUNK, SSM_STATE, LANES), lambda g, i: (0, 0, 0, g)),
                  pl.BlockSpec((1,) + ap.shape[1:], lambda g, i: (g, 0, 0))],
        out_specs=pl.BlockSpec((SSM_ROWS * CHUNK, LANES), lambda g, i: (i, g)),
        out_shape=jax.ShapeDtypeStruct((r * CHUNK, nblk * LANES), jnp.float32),
        scratch_shapes=[mat, mat, mat,
                        pltpu.VMEM((1, BLOCK_STATES), jnp.float32),
                        pltpu.VMEM((1, BLOCK_STATES), jnp.float32)],
        compiler_params=pltpu.CompilerParams(dimension_semantics=("arbitrary", "arbitrary"),
                                             vmem_limit_bytes=VMEM_LIMIT),
        name="ssm_chunk",
    )(u4, kb, fb, mb, ap)


def _main_kernel(x_ref, y_ref, pre_g_ref, w_ref, cw_ref, cb_ref, lng_ref, lnb_ref, wco_ref,
                 wglu_ref, bglu_ref, wso_ref, wout_ref, post_g_ref, o_ref,
                 h_ref, glu_ref, cs_ref, cv_ref, pz_ref, so_ref, a_ref, acc_ref, yg_ref, yb_ref,
                 lin_ref, *, tile, d, ds):
    bf16, f32 = jnp.bfloat16, jnp.float32
    o_zc = 2 * d
    p_zs, p_gc, p_gs, p_end = d, d + ds, 2 * d + ds, 3 * d + ds
    nlane = d // LANES
    pieces = [slice(r, r + EW_ROWS) for r in range(0, tile, EW_ROWS)]

    @pl.when(pl.program_id(1) == 0)
    def _():
        cs_ref[:, 0:CONV_HALO, :] = jnp.zeros((nlane, CONV_HALO, LANES), f32)

    for rows in pieces:
        h_ref[rows, :] = (_rms_scale(x_ref[0, rows, :]) * pre_g_ref[...]).astype(bf16)

    glu_ref[...] = jnp.dot(h_ref[...], w_ref[:, 0:o_zc], preferred_element_type=f32)
    for rows in pieces:
        cu = glu_ref[rows, 0:d] * _sigmoid(glu_ref[rows, d:2 * d])
        for j in range(nlane):
            cs_ref[j, CONV_HALO + rows.start:CONV_HALO + rows.stop, :] = cu[:, j * LANES:(j + 1) * LANES]

    def ssm_glu():
        for rows in pieces:
            yg = _gelu_tanh(y_ref[0, rows, :])
            yg_ref[rows, :] = yg
            yb_ref[rows, :] = yg.astype(bf16)
        lin_ref[...] = jnp.dot(yb_ref[...], wglu_ref[...], preferred_element_type=f32)

    def ssm_out():
        for rows in pieces:
            gate = _sigmoid(lin_ref[rows, :] + bglu_ref[...]) * _silu(pz_ref[rows, p_zs:p_gc])
            yb_ref[rows, :] = (yg_ref[rows, :] * gate).astype(bf16)
        so_ref[...] = jnp.dot(yb_ref[...], wso_ref[...], preferred_element_type=f32)

    def conv_lane_block(j, lanes):
        for r in range(tile // CONV_ROWS):
            acc = jnp.broadcast_to(cb_ref[:, lanes], (CONV_ROWS, LANES))
            for lag in range(CONV_SIZE):
                lo = CONV_HALO + r * CONV_ROWS - lag
                tap = cw_ref[CONV_SIZE - 1 - lag:CONV_SIZE - lag, lanes]
                acc = acc + tap * cs_ref[j, lo:lo + CONV_ROWS, :]
            cv_ref[r * CONV_ROWS:(r + 1) * CONV_ROWS, lanes] = acc

    step_cols = MXU_COLS * -(-p_end // (MXU_COLS * (nlane // CONV_BLOCKS - 1)))
    n_steps = -(-p_end // step_cols)

    def conv_and_proj(i, carry):
        col = pl.multiple_of(i * step_cols, MXU_COLS)
        pz_ref[:, pl.ds(col, step_cols)] = jnp.dot(
            h_ref[...], w_ref[:, pl.ds(o_zc + col, step_cols)], preferred_element_type=f32)
        for q in range(CONV_BLOCKS):
            j = i * CONV_BLOCKS + q
            conv_lane_block(j, pl.ds(pl.multiple_of(j * LANES, LANES), LANES))
        return carry

    assert n_steps * step_cols == p_end and n_steps * CONV_BLOCKS < nlane
    lax.fori_loop(0, n_steps, conv_and_proj, 0)
    ssm_glu()
    for j in range(n_steps * CONV_BLOCKS, nlane):
        conv_lane_block(j, slice(j * LANES, (j + 1) * LANES))
    ssm_out()
    cs_ref[:, 0:CONV_HALO, :] = cs_ref[:, tile:CONV_HALO + tile, :]

    for rows in pieces:
        cv = cv_ref[rows, :]
        xc = cv - jnp.mean(cv, axis=-1, keepdims=True)
        var = jnp.mean(xc * xc, axis=-1, keepdims=True)
        ln = xc * lax.rsqrt(var + LN_EPS) * lng_ref[...] + lnb_ref[...]
        a_ref[rows, :] = (_silu(ln) * _silu(pz_ref[rows, 0:p_zs])).astype(bf16)
    acc_ref[...] = jnp.dot(a_ref[...], wco_ref[...], preferred_element_type=f32)

    for rows in pieces:
        merged = (_sigmoid(pz_ref[rows, p_gc:p_gs]) * acc_ref[rows, :]
                  + _sigmoid(pz_ref[rows, p_gs:p_end]) * so_ref[rows, :])
        a_ref[rows, :] = merged.astype(bf16)
    acc_ref[...] = jnp.dot(a_ref[...], wout_ref[...], preferred_element_type=f32)
    for rows in pieces:
        o_ref[0, rows, :] = x_ref[0, rows, :] + _rms_scale(acc_ref[rows, :]) * post_g_ref[...]


def _main(x, y, pre_g, w_rest, cw, cb, lng, lnb, wco, wglu, bglu, wso, wout, post_g):
    b, l, d = x.shape
    ds = y.shape[-1]
    tile = MAIN_TILE
    assert CONV_HALO >= CONV_SIZE - 1 and CONV_HALO % SUBLANES == 0 and tile >= CONV_HALO
    kern = functools.partial(_main_kernel, tile=tile, d=d, ds=ds)
    consts = (pre_g, w_rest, cw, cb, lng, lnb, wco, wglu, bglu, wso, wout, post_g)
    return pl.pallas_call(
        kern,
        grid=(b, l // tile),
        in_specs=[pl.BlockSpec((1, tile, d), lambda i, j: (i, j, 0)),
                  pl.BlockSpec((1, tile, ds), lambda i, j: (i, j, 0))]
                 + [_const_spec(a.shape) for a in consts],
        out_specs=pl.BlockSpec((1, tile, d), lambda i, j: (i, j, 0)),
        out_shape=jax.ShapeDtypeStruct((b, l, d), x.dtype),
        scratch_shapes=[pltpu.VMEM((tile, d), jnp.bfloat16),
                        pltpu.VMEM((tile, 2 * d), jnp.float32),
                        pltpu.VMEM((d // LANES, CONV_HALO + tile, LANES), jnp.float32),
                        pltpu.VMEM((tile, d), jnp.float32),
                        pltpu.VMEM((tile, 3 * d + ds), jnp.float32),
                        pltpu.VMEM((tile, d), jnp.float32),
                        pltpu.VMEM((tile, d), jnp.bfloat16),
                        pltpu.VMEM((tile, d), jnp.float32),
                        pltpu.VMEM((tile, ds), jnp.float32),
                        pltpu.VMEM((tile, ds), jnp.bfloat16),
                        pltpu.VMEM((tile, ds), jnp.float32)],
        compiler_params=pltpu.CompilerParams(dimension_semantics=("arbitrary", "arbitrary"),
                                             vmem_limit_bytes=VMEM_LIMIT),
        name="main_block",
    )(x, y, *consts)


def _layer(x, pre_g, w_in, conv_w, conv_b, ln_g, ln_b, w_conv_out, lam_re, lam_im, log_dt,
           b_re, b_im, c_re, c_im, d_skip, w_glu, b_glu, w_ssm_out, w_out, post_g):
    bsz, length, d = x.shape
    ds = w_glu.shape[0]
    bf16 = jnp.bfloat16
    o_us = 3 * d
    w_us = w_in[:, o_us:o_us + ds].astype(bf16)
    w_rest = jnp.concatenate([w_in[:, :o_us], w_in[:, o_us + ds:]], axis=1).astype(bf16)
    row = lambda v: v.reshape(1, -1)
    assert lam_re.shape == (ds // SSM_GROUP, SSM_STATE)

    u4 = _us_proj(x.reshape(bsz * length, d), row(pre_g), w_us)
    n_levels = SSM_ROWS.bit_length() - 1
    kb, fb, mb, ap = _ssm_compact(lam_re, lam_im, log_dt, b_re, b_im, c_re, c_im, d_skip, n_levels)
    y = _ssm_chunk(u4, kb, fb, mb, ap, length // CHUNK).reshape(bsz, length, ds)

    cw = jnp.concatenate([conv_w, jnp.zeros((SUBLANES - CONV_SIZE % SUBLANES, d), conv_w.dtype)], axis=0)
    return _main(x, y, row(pre_g), w_rest, cw, row(conv_b), row(ln_g), row(ln_b),
                 w_conv_out.astype(bf16), w_glu.astype(bf16), row(b_glu), w_ssm_out.astype(bf16),
                 w_out.astype(bf16), row(post_g))


def kernel(x, pre_norm_gain, w_in, conv_w, conv_b, conv_ln_gain, conv_ln_bias, w_conv_out, ssm_lambda_re, ssm_lambda_im, ssm_log_dt, ssm_b_re, ssm_b_im, ssm_c_re, ssm_c_im, ssm_d, w_ssm_glu, b_ssm_glu, w_ssm_out, w_out, post_norm_gain):
    for l in range(pre_norm_gain.shape[0]):
        x = _layer(x, pre_norm_gain[l], w_in[l], conv_w[l], conv_b[l], conv_ln_gain[l],
                   conv_ln_bias[l], w_conv_out[l], ssm_lambda_re[l], ssm_lambda_im[l],
                   ssm_log_dt[l], ssm_b_re[l], ssm_b_im[l], ssm_c_re[l], ssm_c_im[l], ssm_d[l],
                   w_ssm_glu[l], b_ssm_glu[l], w_ssm_out[l], w_out[l], post_norm_gain[l])
    return x
```

```python
import functools
import math

import jax
import jax.numpy as jnp
from jax import lax
from jax.experimental import pallas as pl
from jax.experimental.pallas import tpu as pltpu

RMS_EPS = 1e-6
LN_EPS = 1e-5
CONV_SIZE = 31
SSM_GROUP = 16
SSM_STATE = 64
CHUNK = 8
LANES = 128
SUBLANES = 8
GROUPS_PER_BLOCK = LANES // SSM_GROUP
BLOCK_STATES = GROUPS_PER_BLOCK * SSM_STATE
ROW_WIDTH = CHUNK * LANES

US_TILE = 1024
SSM_ROWS = 512
MAIN_TILE = 512
CONV_HALO = 32
CONV_ROWS = 64
MXU_COLS = 256
EW_ROWS = 16
VMEM_LIMIT = 56 * 1024 * 1024


def _sigmoid(x):
    return 0.5 * jnp.tanh(0.5 * x) + 0.5


def _silu(x):
    h = 0.5 * x
    return h * jnp.tanh(h) + h


def _gelu_tanh(x):
    c = math.sqrt(2.0 / math.pi)
    return 0.5 * x * (1.0 + jnp.tanh(c * (x + 0.044715 * (x * x * x))))


def _rms_scale(x):
    return x * lax.rsqrt(jnp.mean(x * x, axis=-1, keepdims=True) + RMS_EPS)


def _const_spec(shape):
    zeros = (0,) * len(shape)
    return pl.BlockSpec(shape, lambda *_: zeros)


def _us_proj_kernel(x_ref, g_ref, w_ref, u_ref, slab_ref, *, tile, nblk):
    h = _rms_scale(x_ref[...]) * g_ref[...]
    u = jnp.dot(h.astype(jnp.bfloat16), w_ref[...], preferred_element_type=jnp.float32)
    for b in range(nblk):
        slab_ref[b] = u[:, b * LANES:(b + 1) * LANES]
    for b in range(nblk):
        for k in range(CHUNK):
            u_ref[b, :, k * LANES:(k + 1) * LANES] = slab_ref[
                b, pl.ds(k, tile // CHUNK, stride=CHUNK), :].astype(u_ref.dtype)


def _us_proj(x2, pre_g, w_in, col, n):
    t, d = x2.shape
    nblk = n // LANES
    assert col % n == 0
    kern = functools.partial(_us_proj_kernel, tile=US_TILE, nblk=nblk)
    return pl.pallas_call(
        kern,
        grid=(t // US_TILE,),
        in_specs=[pl.BlockSpec((US_TILE, d), lambda i: (i, 0)),
                  _const_spec((1, d)),
                  pl.BlockSpec((d, n), lambda i: (0, col // n))],
        out_specs=pl.BlockSpec((nblk, US_TILE // CHUNK, ROW_WIDTH), lambda i: (0, i, 0)),
        out_shape=jax.ShapeDtypeStruct((nblk, t // CHUNK, ROW_WIDTH), jnp.bfloat16),
        scratch_shapes=[pltpu.VMEM((nblk, US_TILE, LANES), jnp.float32)],
        compiler_params=pltpu.CompilerParams(dimension_semantics=("arbitrary",),
                                             vmem_limit_bytes=VMEM_LIMIT),
        name="us_proj",
    )(x2, pre_g, w_in)


def _ssm_compact(lam_re, lam_im, log_dt, b_re, b_im, c_re, c_im, d, n_levels):
    groups, states = lam_re.shape
    gh = groups * SSM_GROUP
    dt = jnp.exp(log_dt)[:, None]
    lr, li = lam_re, lam_im
    mag = jnp.exp(lr * dt)
    ar = mag * jnp.cos(li * dt)
    ai = mag * jnp.sin(li * dt)
    den = lr * lr + li * li
    zr = ((ar - 1.0) * lr + ai * li) / den
    zi = (ai * lr - (ar - 1.0) * li) / den
    bbr = zr[..., None] * b_re - zi[..., None] * b_im
    bbi = zr[..., None] * b_im + zi[..., None] * b_re
    pr, pi = [jnp.ones_like(ar)], [jnp.zeros_like(ai)]
    for _ in range(CHUNK):
        pr.append(pr[-1] * ar - pi[-1] * ai)
        pi.append(pr[-2] * ai + pi[-1] * ar)
    pr, pi = jnp.stack(pr), jnp.stack(pi)

    bt_r = bbr.transpose(2, 0, 1).reshape(1, SSM_GROUP, groups * states)
    bt_i = bbi.transpose(2, 0, 1).reshape(1, SSM_GROUP, groups * states)
    qr = pr[CHUNK - 1::-1].reshape(CHUNK, 1, groups * states)
    qi = pi[CHUNK - 1::-1].reshape(CHUNK, 1, groups * states)
    fb = jnp.stack([qr * bt_r - qi * bt_i, qr * bt_i + qi * bt_r])

    spread = lambda v: jnp.repeat(jnp.swapaxes(v, -1, -2), SSM_GROUP, axis=-1)
    px_r, px_i = spread(pr), spread(pi)
    ct_r = c_re.transpose(2, 0, 1).reshape(1, states, gh)
    ct_i = c_im.transpose(2, 0, 1).reshape(1, states, gh)
    m_r = ct_r * px_r - ct_i * px_i
    m_i = ct_r * px_i + ct_i * px_r
    mb = jnp.stack([m_r[1:], -m_i[1:]])
    bx_r = spread(bbr.transpose(2, 0, 1))
    bx_i = spread(bbi.transpose(2, 0, 1))
    kb = jnp.sum(m_r[:CHUNK, None] * bx_r[None] - m_i[:CHUNK, None] * bx_i[None], axis=2)
    skip = (d.T[:, :, None] * jnp.eye(SSM_GROUP, dtype=d.dtype)[:, None, :]).reshape(SSM_GROUP, gh)
    kb = kb.at[0].add(skip)

    lev_r, lev_i = [pr[CHUNK]], [pi[CHUNK]]
    for _ in range(n_levels - 1):
        lev_r.append(lev_r[-1] * lev_r[-1] - lev_i[-1] * lev_i[-1])
        lev_i.append(2.0 * lev_r[-2] * lev_i[-1])
    ap = jnp.stack([v for pair in zip(lev_r, lev_i) for v in pair])
    ap = ap.reshape(2 * n_levels, groups // GROUPS_PER_BLOCK, BLOCK_STATES).transpose(1, 0, 2)
    return kb, fb, mb, ap


def _block_diag(compact, group_rows, group_cols):
    n = compact.shape[1]
    tiled = jnp.concatenate([compact] * GROUPS_PER_BLOCK, axis=0)
    r = lax.broadcasted_iota(jnp.int32, tiled.shape, 0) // group_rows
    c = lax.broadcasted_iota(jnp.int32, tiled.shape, 1) // group_cols
    return jnp.where(r == c, tiled, 0.0).astype(jnp.bfloat16)


def _ssm_chunk_kernel(u_ref, kb_ref, fb_ref, mb_ref, ap_ref, y_ref,
                      tz_ref, wb_ref, wc_ref, cre_ref, cim_ref, *, rows, n_levels, blocks_per_seq):
    half = BLOCK_STATES

    @pl.when(pl.program_id(1) == 0)
    def _():
        tz_ref[...] = jnp.zeros_like(tz_ref)
        for lag in range(CHUNK):
            blk = _block_diag(kb_ref[lag], SSM_GROUP, SSM_GROUP)
            for k in range(CHUNK - lag):
                t = k + lag
                tz_ref[k * LANES:(k + 1) * LANES, t * LANES:(t + 1) * LANES] = blk
        for ri in range(2):
            for k in range(CHUNK):
                wb_ref[k * LANES:(k + 1) * LANES, ri * half:(ri + 1) * half] = _block_diag(
                    fb_ref[ri, k], SSM_GROUP, SSM_STATE)
                wc_ref[ri * half:(ri + 1) * half, k * LANES:(k + 1) * LANES] = _block_diag(
                    mb_ref[ri, k], SSM_STATE, SSM_GROUP)

    @pl.when(pl.program_id(1) % blocks_per_seq == 0)
    def _():
        cre_ref[...] = jnp.zeros_like(cre_ref)
        cim_ref[...] = jnp.zeros_like(cim_ref)

    u = u_ref[0]
    y1 = jnp.dot(u, tz_ref[...], preferred_element_type=jnp.float32)
    bc = jnp.dot(u, wb_ref[...], preferred_element_type=jnp.float32)
    sre, sim = bc[:, :half], bc[:, half:]
    row = lax.broadcasted_iota(jnp.int32, (rows, 1), 0)
    first = row == 0
    cre, cim = cre_ref[...], cim_ref[...]
    a8r, a8i = ap_ref[0, 0:1, :], ap_ref[0, 1:2, :]
    sre = sre + jnp.where(first, a8r * cre - a8i * cim, 0.0)
    sim = sim + jnp.where(first, a8r * cim + a8i * cre, 0.0)
    for lev in range(n_levels):
        s = 1 << lev
        mr, mi = ap_ref[0, 2 * lev:2 * lev + 1, :], ap_ref[0, 2 * lev + 1:2 * lev + 2, :]
        keep = row >= s
        shr = jnp.where(keep, pltpu.roll(sre, s, 0), 0.0)
        shi = jnp.where(keep, pltpu.roll(sim, s, 0), 0.0)
        sre, sim = sre + (mr * shr - mi * shi), sim + (mr * shi + mi * shr)
    st_re = jnp.where(first, cre, pltpu.roll(sre, 1, 0))
    st_im = jnp.where(first, cim, pltpu.roll(sim, 1, 0))
    cre_ref[...] = sre[rows - 1:rows, :]
    cim_ref[...] = sim[rows - 1:rows, :]
    sst = jnp.concatenate([st_re, st_im], axis=1).astype(jnp.bfloat16)
    y = y1 + jnp.dot(sst, wc_ref[...], preferred_element_type=jnp.float32)
    for t in range(CHUNK):
        y_ref[pl.ds(t, rows, stride=CHUNK), :] = y[:, t * LANES:(t + 1) * LANES]


def _ssm_chunk(u4, kb, fb, mb, ap, rows_per_seq):
    nblk, r, size = u4.shape
    n_levels = ap.shape[1] // 2
    assert SSM_ROWS == 1 << n_levels and rows_per_seq % SSM_ROWS == 0 and size == ROW_WIDTH
    kern = functools.partial(_ssm_chunk_kernel, rows=SSM_ROWS, n_levels=n_levels,
                             blocks_per_seq=rows_per_seq // SSM_ROWS)
    mat = pltpu.VMEM((ROW_WIDTH, ROW_WIDTH), jnp.bfloat16)
    assert 2 * BLOCK_STATES == ROW_WIDTH
    return pl.pallas_call(
        kern,
        grid=(nblk, r // SSM_ROWS),
        in_specs=[pl.BlockSpec((1, SSM_ROWS, size), lambda g, i: (g, i, 0)),
                  pl.BlockSpec((CHUNK, SSM_GROUP, LANES), lambda g, i: (0, 0, g)),
                  pl.BlockSpec((2, CHUNK, SSM_GROUP, BLOCK_STATES), lambda g, i: (0, 0, 0, g)),
                  pl.BlockSpec((2, CHUNK, SSM_STATE, LANES), lambda g, i: (0, 0, 0, g)),
                  pl.BlockSpec((1,) + ap.shape[1:], lambda g, i: (g, 0, 0))],
        out_specs=pl.BlockSpec((SSM_ROWS * CHUNK, LANES), lambda g, i: (i, g)),
        out_shape=jax.ShapeDtypeStruct((r * CHUNK, nblk * LANES), jnp.float32),
        scratch_shapes=[mat, mat, mat,
                        pltpu.VMEM((1, BLOCK_STATES), jnp.float32),
                        pltpu.VMEM((1, BLOCK_STATES), jnp.float32)],
        compiler_params=pltpu.CompilerParams(dimension_semantics=("arbitrary", "arbitrary"),
                                             vmem_limit_bytes=VMEM_LIMIT),
        name="ssm_chunk",
    )(u4, kb, fb, mb, ap)


def _main_kernel(x_ref, y_ref, pre_g_ref, w_ref, cw_ref, cb_ref, lng_ref, lnb_ref, wco_ref,
                 wglu_ref, bglu_ref, wso_ref, wout_ref, post_g_ref, o_ref,
                 h_ref, glu_ref, cs_ref, cv_ref, pz_ref, so_ref, a_ref, acc_ref, yg_ref, yb_ref,
                 lin_ref, *, tile, d, ds):
    bf16, f32 = jnp.bfloat16, jnp.float32
    o_zc = 2 * d
    p_zs, p_gc, p_gs, p_end = d, d + ds, 2 * d + ds, 3 * d + ds
    nlane = d // LANES
    pieces = [slice(r, r + EW_ROWS) for r in range(0, tile, EW_ROWS)]

    @pl.when(pl.program_id(1) == 0)
    def _():
        cs_ref[:, 0:CONV_HALO, :] = jnp.zeros((nlane, CONV_HALO, LANES), f32)

    for rows in pieces:
        h_ref[rows, :] = (_rms_scale(x_ref[0, rows, :]) * pre_g_ref[...]).astype(bf16)

    glu_ref[...] = jnp.dot(h_ref[...], w_ref[:, 0:o_zc], preferred_element_type=f32)
    for rows in pieces:
        cu = glu_ref[rows, 0:d] * _sigmoid(glu_ref[rows, d:2 * d])
        for j in range(nlane):
            cs_ref[j, CONV_HALO + rows.start:CONV_HALO + rows.stop, :] = cu[:, j * LANES:(j + 1) * LANES]

    def ssm_glu():
        for rows in pieces:
            yg = _gelu_tanh(y_ref[0, rows, :])
            yg_ref[rows, :] = yg
            yb_ref[rows, :] = yg.astype(bf16)
        lin_ref[...] = jnp.dot(yb_ref[...], wglu_ref[...], preferred_element_type=f32)

    def ssm_out():
        for rows in pieces:
            gate = _sigmoid(lin_ref[rows, :] + bglu_ref[...]) * _silu(pz_ref[rows, p_zs:p_gc])
            yb_ref[rows, :] = (yg_ref[rows, :] * gate).astype(bf16)
        so_ref[...] = jnp.dot(yb_ref[...], wso_ref[...], preferred_element_type=f32)

    def conv_lane_block(j, lanes):
        for r in range(tile // CONV_ROWS):
            acc = jnp.broadcast_to(cb_ref[:, lanes], (CONV_ROWS, LANES))
            for lag in range(CONV_SIZE):
                lo = CONV_HALO + r * CONV_ROWS - lag
                tap = cw_ref[CONV_SIZE - 1 - lag:CONV_SIZE - lag, lanes]
                acc = acc + tap * cs_ref[j, lo:lo + CONV_ROWS, :]
            cv_ref[r * CONV_ROWS:(r + 1) * CONV_ROWS, lanes] = acc

    pz_ref[:, 0:p_zs] = jnp.dot(h_ref[...], w_ref[:, o_zc:o_zc + p_zs], preferred_element_type=f32)
    pz_ref[:, p_zs:p_end] = jnp.dot(h_ref[...], w_ref[:, o_zc + p_zs + ds:o_zc + p_end + ds],
                                    preferred_element_type=f32)
    ssm_glu()
    ssm_out()

    def conv_step(j, carry):
        conv_lane_block(j, pl.ds(pl.multiple_of(j * LANES, LANES), LANES))
        return carry

    lax.fori_loop(0, nlane, conv_step, 0)
    cs_ref[:, 0:CONV_HALO, :] = cs_ref[:, tile:CONV_HALO + tile, :]

    for rows in pieces:
        cv = cv_ref[rows, :]
        xc = cv - jnp.mean(cv, axis=-1, keepdims=True)
        var = jnp.mean(xc * xc, axis=-1, keepdims=True)
        ln = xc * lax.rsqrt(var + LN_EPS) * lng_ref[...] + lnb_ref[...]
        a_ref[rows, :] = (_silu(ln) * _silu(pz_ref[rows, 0:p_zs])).astype(bf16)
    acc_ref[...] = jnp.dot(a_ref[...], wco_ref[...], preferred_element_type=f32)

    for rows in pieces:
        merged = (_sigmoid(pz_ref[rows, p_gc:p_gs]) * acc_ref[rows, :]
                  + _sigmoid(pz_ref[rows, p_gs:p_end]) * so_ref[rows, :])
        a_ref[rows, :] = merged.astype(bf16)
    acc_ref[...] = jnp.dot(a_ref[...], wout_ref[...], preferred_element_type=f32)
    for rows in pieces:
        o_ref[0, rows, :] = x_ref[0, rows, :] + _rms_scale(acc_ref[rows, :]) * post_g_ref[...]


def _main(x, y, pre_g, w_in, cw, cb, lng, lnb, wco, wglu, bglu, wso, wout, post_g):
    b, l, d = x.shape
    ds = y.shape[-1]
    tile = MAIN_TILE
    assert CONV_HALO >= CONV_SIZE - 1 and CONV_HALO % SUBLANES == 0 and tile >= CONV_HALO
    kern = functools.partial(_main_kernel, tile=tile, d=d, ds=ds)
    consts = (pre_g, w_in, cw, cb, lng, lnb, wco, wglu, bglu, wso, wout, post_g)
    return pl.pallas_call(
        kern,
        grid=(b, l // tile),
        in_specs=[pl.BlockSpec((1, tile, d), lambda i, j: (i, j, 0)),
                  pl.BlockSpec((1, tile, ds), lambda i, j: (i, j, 0))]
                 + [_const_spec(a.shape) for a in consts],
        out_specs=pl.BlockSpec((1, tile, d), lambda i, j: (i, j, 0)),
        out_shape=jax.ShapeDtypeStruct((b, l, d), x.dtype),
        scratch_shapes=[pltpu.VMEM((tile, d), jnp.bfloat16),
                        pltpu.VMEM((tile, 2 * d), jnp.float32),
                        pltpu.VMEM((d // LANES, CONV_HALO + tile, LANES), jnp.float32),
                        pltpu.VMEM((tile, d), jnp.float32),
                        pltpu.VMEM((tile, 3 * d + ds), jnp.float32),
                        pltpu.VMEM((tile, d), jnp.float32),
                        pltpu.VMEM((tile, d), jnp.bfloat16),
                        pltpu.VMEM((tile, d), jnp.float32),
                        pltpu.VMEM((tile, ds), jnp.float32),
                        pltpu.VMEM((tile, ds), jnp.bfloat16),
                        pltpu.VMEM((tile, ds), jnp.float32)],
        compiler_params=pltpu.CompilerParams(dimension_semantics=("arbitrary", "arbitrary"),
                                             vmem_limit_bytes=VMEM_LIMIT),
        name="main_block",
    )(x, y, *consts)


def _layer(x, pre_g, w_in, conv_w, conv_b, ln_g, ln_b, w_conv_out, lam_re, lam_im, log_dt,
           b_re, b_im, c_re, c_im, d_skip, w_glu, b_glu, w_ssm_out, w_out, post_g):
    bsz, length, d = x.shape
    ds = w_glu.shape[0]
    bf16 = jnp.bfloat16
    o_us = 3 * d
    w_bf = w_in.astype(bf16)
    row = lambda v: v.reshape(1, -1)
    assert lam_re.shape == (ds // SSM_GROUP, SSM_STATE)

    u4 = _us_proj(x.reshape(bsz * length, d), row(pre_g), w_bf, o_us, ds)
    n_levels = SSM_ROWS.bit_length() - 1
    kb, fb, mb, ap = _ssm_compact(lam_re, lam_im, log_dt, b_re, b_im, c_re, c_im, d_skip, n_levels)
    y = _ssm_chunk(u4, kb, fb, mb, ap, length // CHUNK).reshape(bsz, length, ds)

    cw = jnp.concatenate([conv_w, jnp.zeros((SUBLANES - CONV_SIZE % SUBLANES, d), conv_w.dtype)], axis=0)
    return _main(x, y, row(pre_g), w_bf, cw, row(conv_b), row(ln_g), row(ln_b),
                 w_conv_out.astype(bf16), w_glu.astype(bf16), row(b_glu), w_ssm_out.astype(bf16),
                 w_out.astype(bf16), row(post_g))


def kernel(x, pre_norm_gain, w_in, conv_w, conv_b, conv_ln_gain, conv_ln_bias, w_conv_out, ssm_lambda_re, ssm_lambda_im, ssm_log_dt, ssm_b_re, ssm_b_im, ssm_c_re, ssm_c_im, ssm_d, w_ssm_glu, b_ssm_glu, w_ssm_out, w_out, post_norm_gain):
    for l in range(pre_norm_gain.shape[0]):
        x = _layer(x, pre_norm_gain[l], w_in[l], conv_w[l], conv_b[l], conv_ln_gain[l],
                   conv_ln_bias[l], w_conv_out[l], ssm_lambda_re[l], ssm_lambda_im[l],
                   ssm_log_dt[l], ssm_b_re[l], ssm_b_im[l], ssm_c_re[l], ssm_c_im[l], ssm_d[l],
                   w_ssm_glu[l], b_ssm_glu[l], w_ssm_out[l], w_out[l], post_norm_gain[l])
    return x
```

```python
import functools
import math

import jax
import jax.numpy as jnp
from jax import lax
from jax.experimental import pallas as pl
from jax.experimental.pallas import tpu as pltpu

RMS_EPS = 1e-6
LN_EPS = 1e-5
CONV_SIZE = 31
SSM_GROUP = 16
SSM_STATE = 64
CHUNK = 8
LANES = 128
SUBLANES = 8
GROUPS_PER_BLOCK = LANES // SSM_GROUP
BLOCK_STATES = GROUPS_PER_BLOCK * SSM_STATE
ROW_WIDTH = CHUNK * LANES

US_TILE = 2048
SSM_ROWS = 512
MAIN_TILE = 512
CONV_HALO = 32
CONV_ROWS = 64
MXU_COLS = 256
EW_ROWS = 16
VMEM_LIMIT = 56 * 1024 * 1024


def _sigmoid(x):
    return 0.5 * jnp.tanh(0.5 * x) + 0.5


def _silu(x):
    h = 0.5 * x
    return h * jnp.tanh(h) + h


def _gelu_tanh(x):
    c = math.sqrt(2.0 / math.pi)
    return 0.5 * x * (1.0 + jnp.tanh(c * (x + 0.044715 * (x * x * x))))


def _rms_scale(x):
    return x * lax.rsqrt(jnp.mean(x * x, axis=-1, keepdims=True) + RMS_EPS)


def _const_spec(shape):
    zeros = (0,) * len(shape)
    return pl.BlockSpec(shape, lambda *_: zeros)


def _us_proj_kernel(x_ref, g_ref, w_ref, u_ref, slab_ref, *, tile, nblk):
    h = _rms_scale(x_ref[...]) * g_ref[...]
    u = jnp.dot(h.astype(jnp.bfloat16), w_ref[...], preferred_element_type=jnp.float32)
    for b in range(nblk):
        slab_ref[b] = u[:, b * LANES:(b + 1) * LANES]
    for b in range(nblk):
        for k in range(CHUNK):
            u_ref[b, :, k * LANES:(k + 1) * LANES] = slab_ref[
                b, pl.ds(k, tile // CHUNK, stride=CHUNK), :].astype(u_ref.dtype)


def _us_proj(x2, pre_g, w_in, col, n):
    t, d = x2.shape
    nblk = n // LANES
    assert col % n == 0
    kern = functools.partial(_us_proj_kernel, tile=US_TILE, nblk=nblk)
    return pl.pallas_call(
        kern,
        grid=(t // US_TILE,),
        in_specs=[pl.BlockSpec((US_TILE, d), lambda i: (i, 0)),
                  _const_spec((1, d)),
                  pl.BlockSpec((d, n), lambda i: (0, col // n))],
        out_specs=pl.BlockSpec((nblk, US_TILE // CHUNK, ROW_WIDTH), lambda i: (0, i, 0)),
        out_shape=jax.ShapeDtypeStruct((nblk, t // CHUNK, ROW_WIDTH), jnp.bfloat16),
        scratch_shapes=[pltpu.VMEM((nblk, US_TILE, LANES), jnp.float32)],
        compiler_params=pltpu.CompilerParams(dimension_semantics=("arbitrary",),
                                             vmem_limit_bytes=VMEM_LIMIT),
        name="us_proj",
    )(x2, pre_g, w_in)


def _ssm_compact(lam_re, lam_im, log_dt, b_re, b_im, c_re, c_im, d, n_levels):
    groups, states = lam_re.shape
    gh = groups * SSM_GROUP
    dt = jnp.exp(log_dt)[:, None]
    lr, li = lam_re, lam_im
    mag = jnp.exp(lr * dt)
    ar = mag * jnp.cos(li * dt)
    ai = mag * jnp.sin(li * dt)
    den = lr * lr + li * li
    zr = ((ar - 1.0) * lr + ai * li) / den
    zi = (ai * lr - (ar - 1.0) * li) / den
    bbr = zr[..., None] * b_re - zi[..., None] * b_im
    bbi = zr[..., None] * b_im + zi[..., None] * b_re
    pr, pi = [jnp.ones_like(ar)], [jnp.zeros_like(ai)]
    for _ in range(CHUNK):
        pr.append(pr[-1] * ar - pi[-1] * ai)
        pi.append(pr[-2] * ai + pi[-1] * ar)
    pr, pi = jnp.stack(pr), jnp.stack(pi)

    bt_r = bbr.transpose(2, 0, 1).reshape(1, SSM_GROUP, groups * states)
    bt_i = bbi.transpose(2, 0, 1).reshape(1, SSM_GROUP, groups * states)
    qr = pr[CHUNK - 1::-1].reshape(CHUNK, 1, groups * states)
    qi = pi[CHUNK - 1::-1].reshape(CHUNK, 1, groups * states)
    fb = jnp.stack([qr * bt_r - qi * bt_i, qr * bt_i + qi * bt_r])

    spread = lambda v: jnp.repeat(jnp.swapaxes(v, -1, -2), SSM_GROUP, axis=-1)
    px_r, px_i = spread(pr), spread(pi)
    ct_r = c_re.transpose(2, 0, 1).reshape(1, states, gh)
    ct_i = c_im.transpose(2, 0, 1).reshape(1, states, gh)
    m_r = ct_r * px_r - ct_i * px_i
    m_i = ct_r * px_i + ct_i * px_r
    mb = jnp.stack([m_r[1:], -m_i[1:]])
    bx_r = spread(bbr.transpose(2, 0, 1))
    bx_i = spread(bbi.transpose(2, 0, 1))
    kb = jnp.sum(m_r[:CHUNK, None] * bx_r[None] - m_i[:CHUNK, None] * bx_i[None], axis=2)
    skip = (d.T[:, :, None] * jnp.eye(SSM_GROUP, dtype=d.dtype)[:, None, :]).reshape(SSM_GROUP, gh)
    kb = kb.at[0].add(skip)

    lev_r, lev_i = [pr[CHUNK]], [pi[CHUNK]]
    for _ in range(n_levels - 1):
        lev_r.append(lev_r[-1] * lev_r[-1] - lev_i[-1] * lev_i[-1])
        lev_i.append(2.0 * lev_r[-2] * lev_i[-1])
    ap = jnp.stack([v for pair in zip(lev_r, lev_i) for v in pair])
    ap = ap.reshape(2 * n_levels, groups // GROUPS_PER_BLOCK, BLOCK_STATES).transpose(1, 0, 2)
    row_r, row_i = [jnp.ones_like(ar)], [jnp.zeros_like(ai)]
    for _ in range(SUBLANES - 1):
        row_r.append(row_r[-1] * pr[CHUNK] - row_i[-1] * pi[CHUNK])
        row_i.append(row_r[-2] * pi[CHUNK] + row_i[-1] * pr[CHUNK])
    aq = jnp.stack(row_r + row_i)
    aq = aq.reshape(2 * SUBLANES, groups // GROUPS_PER_BLOCK, BLOCK_STATES).transpose(1, 0, 2)
    return kb, fb, mb, ap, aq


def _block_diag(compact, group_rows, group_cols):
    n = compact.shape[1]
    tiled = jnp.concatenate([compact] * GROUPS_PER_BLOCK, axis=0)
    r = lax.broadcasted_iota(jnp.int32, tiled.shape, 0) // group_rows
    c = lax.broadcasted_iota(jnp.int32, tiled.shape, 1) // group_cols
    return jnp.where(r == c, tiled, 0.0).astype(jnp.bfloat16)


def _ssm_chunk_kernel(u_ref, kb_ref, fb_ref, mb_ref, ap_ref, aq_ref, y_ref,
                      tz_ref, wb_ref, wc_ref, cre_ref, cim_ref, lre_ref, lim_ref, ebr_ref, ebi_ref,
                      *, rows, n_levels, blocks_per_seq):
    half = BLOCK_STATES

    @pl.when(pl.program_id(1) == 0)
    def _():
        tz_ref[...] = jnp.zeros_like(tz_ref)
        for lag in range(CHUNK):
            blk = _block_diag(kb_ref[lag], SSM_GROUP, SSM_GROUP)
            for k in range(CHUNK - lag):
                t = k + lag
                tz_ref[k * LANES:(k + 1) * LANES, t * LANES:(t + 1) * LANES] = blk
        for ri in range(2):
            for k in range(CHUNK):
                wb_ref[k * LANES:(k + 1) * LANES, ri * half:(ri + 1) * half] = _block_diag(
                    fb_ref[ri, k], SSM_GROUP, SSM_STATE)
                wc_ref[ri * half:(ri + 1) * half, k * LANES:(k + 1) * LANES] = _block_diag(
                    mb_ref[ri, k], SSM_STATE, SSM_GROUP)

    @pl.when(pl.program_id(1) % blocks_per_seq == 0)
    def _():
        cre_ref[...] = jnp.zeros_like(cre_ref)
        cim_ref[...] = jnp.zeros_like(cim_ref)

    def scan_level(vr, vi, lev, shift, pos):
        mr, mi = ap_ref[0, 2 * lev:2 * lev + 1, :], ap_ref[0, 2 * lev + 1:2 * lev + 2, :]
        keep = pos >= shift
        sr = jnp.where(keep, pltpu.roll(vr, shift, 0), 0.0)
        si = jnp.where(keep, pltpu.roll(vi, shift, 0), 0.0)
        return vr + (mr * sr - mi * si), vi + (mr * si + mi * sr)

    u = u_ref[0]
    y1 = jnp.dot(u, tz_ref[...], preferred_element_type=jnp.float32)
    bc = jnp.dot(u, wb_ref[...], preferred_element_type=jnp.float32)
    n_grp = rows // SUBLANES
    in_levels = SUBLANES.bit_length() - 1
    assert n_grp == 1 << (n_levels - in_levels)

    tiles = lambda v: v.reshape(n_grp, SUBLANES, half)
    lre, lim = tiles(bc[:, :half]), tiles(bc[:, half:])
    rin = lax.broadcasted_iota(jnp.int32, (1, SUBLANES, 1), 1)
    for lev in range(in_levels):
        s = 1 << lev
        mr = jnp.where(rin >= s, ap_ref[0, 2 * lev:2 * lev + 1, :][None], 0.0)
        mi = jnp.where(rin >= s, ap_ref[0, 2 * lev + 1:2 * lev + 2, :][None], 0.0)
        sr, si = pltpu.roll(lre, s, 1), pltpu.roll(lim, s, 1)
        lre, lim = lre + (mr * sr - mi * si), lim + (mr * si + mi * sr)
    lsh_re = jnp.where(rin >= 1, pltpu.roll(lre, 1, 1), 0.0).reshape(rows, half)
    lsh_im = jnp.where(rin >= 1, pltpu.roll(lim, 1, 1), 0.0).reshape(rows, half)
    lre, lim = lre.reshape(rows, half), lim.reshape(rows, half)
    nslab = half // LANES
    slabs = [slice(q * LANES, (q + 1) * LANES) for q in range(nslab)]
    for q in range(nslab):
        lre_ref[q] = lre[:, slabs[q]]
        lim_ref[q] = lim[:, slabs[q]]

    ends = pl.ds(SUBLANES - 1, n_grp, stride=SUBLANES)
    ere = jnp.concatenate([lre_ref[q, ends, :] for q in range(nslab)], axis=1)
    eim = jnp.concatenate([lim_ref[q, ends, :] for q in range(nslab)], axis=1)
    grp = lax.broadcasted_iota(jnp.int32, (n_grp, 1), 0)
    cre, cim = cre_ref[...], cim_ref[...]
    gr, gi = ap_ref[0, 2 * in_levels:2 * in_levels + 1, :], ap_ref[0, 2 * in_levels + 1:2 * in_levels + 2, :]
    ere = ere + jnp.where(grp == 0, gr * cre - gi * cim, 0.0)
    eim = eim + jnp.where(grp == 0, gr * cim + gi * cre, 0.0)
    for lev in range(in_levels, n_levels):
        ere, eim = scan_level(ere, eim, lev, 1 << (lev - in_levels), grp)
    pre = jnp.where(grp == 0, cre, pltpu.roll(ere, 1, 0))
    pim = jnp.where(grp == 0, cim, pltpu.roll(eim, 1, 0))
    cre_ref[...] = ere[n_grp - 1:n_grp, :]
    cim_ref[...] = eim[n_grp - 1:n_grp, :]

    for q in range(nslab):
        for r in range(SUBLANES):
            ebr_ref[q, pl.ds(r, n_grp, stride=SUBLANES), :] = pre[:, slabs[q]]
            ebi_ref[q, pl.ds(r, n_grp, stride=SUBLANES), :] = pim[:, slabs[q]]
    ebr = jnp.concatenate([ebr_ref[q] for q in range(nslab)], axis=1)
    ebi = jnp.concatenate([ebi_ref[q] for q in range(nslab)], axis=1)
    qr = jnp.concatenate([aq_ref[0, 0:SUBLANES, :]] * n_grp, axis=0)
    qi = jnp.concatenate([aq_ref[0, SUBLANES:2 * SUBLANES, :]] * n_grp, axis=0)
    st_re = lsh_re + (qr * ebr - qi * ebi)
    st_im = lsh_im + (qr * ebi + qi * ebr)
    sst = jnp.concatenate([st_re, st_im], axis=1).astype(jnp.bfloat16)
    y = y1 + jnp.dot(sst, wc_ref[...], preferred_element_type=jnp.float32)
    for t in range(CHUNK):
        y_ref[pl.ds(t, rows, stride=CHUNK), :] = y[:, t * LANES:(t + 1) * LANES]


def _ssm_chunk(u4, kb, fb, mb, ap, aq, rows_per_seq):
    nblk, r, size = u4.shape
    n_levels = ap.shape[1] // 2
    assert SSM_ROWS == 1 << n_levels and rows_per_seq % SSM_ROWS == 0 and size == ROW_WIDTH
    kern = functools.partial(_ssm_chunk_kernel, rows=SSM_ROWS, n_levels=n_levels,
                             blocks_per_seq=rows_per_seq // SSM_ROWS)
    mat = pltpu.VMEM((ROW_WIDTH, ROW_WIDTH), jnp.bfloat16)
    state = pltpu.VMEM((BLOCK_STATES // LANES, SSM_ROWS, LANES), jnp.float32)
    assert 2 * BLOCK_STATES == ROW_WIDTH
    return pl.pallas_call(
        kern,
        grid=(nblk, r // SSM_ROWS),
        in_specs=[pl.BlockSpec((1, SSM_ROWS, size), lambda g, i: (g, i, 0)),
                  pl.BlockSpec((CHUNK, SSM_GROUP, LANES), lambda g, i: (0, 0, g)),
                  pl.BlockSpec((2, CHUNK, SSM_GROUP, BLOCK_STATES), lambda g, i: (0, 0, 0, g)),
                  pl.BlockSpec((2, CHUNK, SSM_STATE, LANES), lambda g, i: (0, 0, 0, g)),
                  pl.BlockSpec((1,) + ap.shape[1:], lambda g, i: (g, 0, 0)),
                  pl.BlockSpec((1,) + aq.shape[1:], lambda g, i: (g, 0, 0))],
        out_specs=pl.BlockSpec((SSM_ROWS * CHUNK, LANES), lambda g, i: (i, g)),
        out_shape=jax.ShapeDtypeStruct((r * CHUNK, nblk * LANES), jnp.float32),
        scratch_shapes=[mat, mat, mat,
                        pltpu.VMEM((1, BLOCK_STATES), jnp.float32),
                        pltpu.VMEM((1, BLOCK_STATES), jnp.float32),
                        state, state, state, state],
        compiler_params=pltpu.CompilerParams(dimension_semantics=("arbitrary", "arbitrary"),
                                             vmem_limit_bytes=VMEM_LIMIT),
        name="ssm_chunk",
    )(u4, kb, fb, mb, ap, aq)


def _main_kernel(x_ref, y_ref, pre_g_ref, w_ref, cw_ref, cb_ref, lng_ref, lnb_ref, wco_ref,
                 wglu_ref, bglu_ref, wso_ref, wout_ref, post_g_ref, o_ref,
                 h_ref, glu_ref, cs_ref, cv_ref, pz_ref, so_ref, a_ref, acc_ref, yg_ref, yb_ref,
                 lin_ref, *, tile, d, ds):
    bf16, f32 = jnp.bfloat16, jnp.float32
    o_zc = 2 * d
    p_zs, p_gc, p_gs, p_end = d, d + ds, 2 * d + ds, 3 * d + ds
    nlane = d // LANES
    pieces = [slice(r, r + EW_ROWS) for r in range(0, tile, EW_ROWS)]

    @pl.when(pl.program_id(1) == 0)
    def _():
        cs_ref[:, 0:CONV_HALO, :] = jnp.zeros((nlane, CONV_HALO, LANES), f32)

    for rows in pieces:
        h_ref[rows, :] = (_rms_scale(x_ref[0, rows, :]) * pre_g_ref[...]).astype(bf16)

    glu_ref[...] = jnp.dot(h_ref[...], w_ref[:, 0:o_zc], preferred_element_type=f32)
    for rows in pieces:
        cu = glu_ref[rows, 0:d] * _sigmoid(glu_ref[rows, d:2 * d])
        for j in range(nlane):
            cs_ref[j, CONV_HALO + rows.start:CONV_HALO + rows.stop, :] = cu[:, j * LANES:(j + 1) * LANES]

    def ssm_glu():
        for rows in pieces:
            yg = _gelu_tanh(y_ref[0, rows, :])
            yg_ref[rows, :] = yg
            yb_ref[rows, :] = yg.astype(bf16)
        lin_ref[...] = jnp.dot(yb_ref[...], wglu_ref[...], preferred_element_type=f32)

    def ssm_out():
        for rows in pieces:
            gate = _sigmoid(lin_ref[rows, :] + bglu_ref[...]) * _silu(pz_ref[rows, p_zs:p_gc])
            yb_ref[rows, :] = (yg_ref[rows, :] * gate).astype(bf16)
        so_ref[...] = jnp.dot(yb_ref[...], wso_ref[...], preferred_element_type=f32)

    def conv_lane_block(j, lanes):
        for r in range(tile // CONV_ROWS):
            acc = jnp.broadcast_to(cb_ref[:, lanes], (CONV_ROWS, LANES))
            for lag in range(CONV_SIZE):
                lo = CONV_HALO + r * CONV_ROWS - lag
                tap = cw_ref[CONV_SIZE - 1 - lag:CONV_SIZE - lag, lanes]
                acc = acc + tap * cs_ref[j, lo:lo + CONV_ROWS, :]
            cv_ref[r * CONV_ROWS:(r + 1) * CONV_ROWS, lanes] = acc

    pz_ref[:, 0:p_zs] = jnp.dot(h_ref[...], w_ref[:, o_zc:o_zc + p_zs], preferred_element_type=f32)
    pz_ref[:, p_zs:p_end] = jnp.dot(h_ref[...], w_ref[:, o_zc + p_zs + ds:o_zc + p_end + ds],
                                    preferred_element_type=f32)
    ssm_glu()
    ssm_out()

    def conv_step(j, carry):
        conv_lane_block(j, pl.ds(pl.multiple_of(j * LANES, LANES), LANES))
        return carry

    lax.fori_loop(0, nlane, conv_step, 0)
    cs_ref[:, 0:CONV_HALO, :] = cs_ref[:, tile:CONV_HALO + tile, :]

    for rows in pieces:
        cv = cv_ref[rows, :]
        xc = cv - jnp.mean(cv, axis=-1, keepdims=True)
        var = jnp.mean(xc * xc, axis=-1, keepdims=True)
        ln = xc * lax.rsqrt(var + LN_EPS) * lng_ref[...] + lnb_ref[...]
        a_ref[rows, :] = (_silu(ln) * _silu(pz_ref[rows, 0:p_zs])).astype(bf16)
    acc_ref[...] = jnp.dot(a_ref[...], wco_ref[...], preferred_element_type=f32)

    for rows in pieces:
        merged = (_sigmoid(pz_ref[rows, p_gc:p_gs]) * acc_ref[rows, :]
                  + _sigmoid(pz_ref[rows, p_gs:p_end]) * so_ref[rows, :])
        a_ref[rows, :] = merged.astype(bf16)
    acc_ref[...] = jnp.dot(a_ref[...], wout_ref[...], preferred_element_type=f32)
    for rows in pieces:
        o_ref[0, rows, :] = x_ref[0, rows, :] + _rms_scale(acc_ref[rows, :]) * post_g_ref[...]


def _main(x, y, pre_g, w_in, cw, cb, lng, lnb, wco, wglu, bglu, wso, wout, post_g):
    b, l, d = x.shape
    ds = y.shape[-1]
    tile = MAIN_TILE
    assert CONV_HALO >= CONV_SIZE - 1 and CONV_HALO % SUBLANES == 0 and tile >= CONV_HALO
    kern = functools.partial(_main_kernel, tile=tile, d=d, ds=ds)
    consts = (pre_g, w_in, cw, cb, lng, lnb, wco, wglu, bglu, wso, wout, post_g)
    return pl.pallas_call(
        kern,
        grid=(b, l // tile),
        in_specs=[pl.BlockSpec((1, tile, d), lambda i, j: (i, j, 0)),
                  pl.BlockSpec((1, tile, ds), lambda i, j: (i, j, 0))]
                 + [_const_spec(a.shape) for a in consts],
        out_specs=pl.BlockSpec((1, tile, d), lambda i, j: (i, j, 0)),
        out_shape=jax.ShapeDtypeStruct((b, l, d), x.dtype),
        scratch_shapes=[pltpu.VMEM((tile, d), jnp.bfloat16),
                        pltpu.VMEM((tile, 2 * d), jnp.float32),
                        pltpu.VMEM((d // LANES, CONV_HALO + tile, LANES), jnp.float32),
                        pltpu.VMEM((tile, d), jnp.float32),
                        pltpu.VMEM((tile, 3 * d + ds), jnp.float32),
                        pltpu.VMEM((tile, d), jnp.float32),
                        pltpu.VMEM((tile, d), jnp.bfloat16),
                        pltpu.VMEM((tile, d), jnp.float32),
                        pltpu.VMEM((tile, ds), jnp.float32),
                        pltpu.VMEM((tile, ds), jnp.bfloat16),
                        pltpu.VMEM((tile, ds), jnp.float32)],
        compiler_params=pltpu.CompilerParams(dimension_semantics=("arbitrary", "arbitrary"),
                                             vmem_limit_bytes=VMEM_LIMIT),
        name="main_block",
    )(x, y, *consts)


def _layer(x, pre_g, w_in, conv_w, conv_b, ln_g, ln_b, w_conv_out, lam_re, lam_im, log_dt,
           b_re, b_im, c_re, c_im, d_skip, w_glu, b_glu, w_ssm_out, w_out, post_g):
    bsz, length, d = x.shape
    ds = w_glu.shape[0]
    bf16 = jnp.bfloat16
    o_us = 3 * d
    w_bf = w_in.astype(bf16)
    row = lambda v: v.reshape(1, -1)
    assert lam_re.shape == (ds // SSM_GROUP, SSM_STATE)

    u4 = _us_proj(x.reshape(bsz * length, d), row(pre_g), w_bf, o_us, ds)
    n_levels = SSM_ROWS.bit_length() - 1
    kb, fb, mb, ap, aq = _ssm_compact(lam_re, lam_im, log_dt, b_re, b_im, c_re, c_im, d_skip, n_levels)
    y = _ssm_chunk(u4, kb, fb, mb, ap, aq, length // CHUNK).reshape(bsz, length, ds)

    cw = jnp.concatenate([conv_w, jnp.zeros((SUBLANES - CONV_SIZE % SUBLANES, d), conv_w.dtype)], axis=0)
    return _main(x, y, row(pre_g), w_bf, cw, row(conv_b), row(ln_g), row(ln_b),
                 w_conv_out.astype(bf16), w_glu.astype(bf16), row(b_glu), w_ssm_out.astype(bf16),
                 w_out.astype(bf16), row(post_g))


def kernel(x, pre_norm_gain, w_in, conv_w, conv_b, conv_ln_gain, conv_ln_bias, w_conv_out, ssm_lambda_re, ssm_lambda_im, ssm_log_dt, ssm_b_re, ssm_b_im, ssm_c_re, ssm_c_im, ssm_d, w_ssm_glu, b_ssm_glu, w_ssm_out, w_out, post_norm_gain):
    for l in range(pre_norm_gain.shape[0]):
        x = _layer(x, pre_norm_gain[l], w_in[l], conv_w[l], conv_b[l], conv_ln_gain[l],
                   conv_ln_bias[l], w_conv_out[l], ssm_lambda_re[l], ssm_lambda_im[l],
                   ssm_log_dt[l], ssm_b_re[l], ssm_b_im[l], ssm_c_re[l], ssm_c_im[l], ssm_d[l],
                   w_ssm_glu[l], b_ssm_glu[l], w_ssm_out[l], w_out[l], post_norm_gain[l])
    return x
```

```python
import functools
import math

import jax
import jax.numpy as jnp
from jax import lax
from jax.experimental import pallas as pl
from jax.experimental.pallas import tpu as pltpu

RMS_EPS = 1e-6
LN_EPS = 1e-5
CONV_SIZE = 31
SSM_GROUP = 16
SSM_STATE = 64
CHUNK = 8
LANES = 128
SUBLANES = 8
GROUPS_PER_BLOCK = LANES // SSM_GROUP
BLOCK_STATES = GROUPS_PER_BLOCK * SSM_STATE
ROW_WIDTH = CHUNK * LANES

US_TILE = 2048
SSM_ROWS = 512
MAIN_TILE = 512
CONV_HALO = 32
CONV_ROWS = 64
MXU_COLS = 256
EW_ROWS = 16
VMEM_LIMIT = 56 * 1024 * 1024


def _sigmoid(x):
    return 0.5 * jnp.tanh(0.5 * x) + 0.5


def _silu(x):
    h = 0.5 * x
    return h * jnp.tanh(h) + h


def _gelu_tanh(x):
    c = math.sqrt(2.0 / math.pi)
    return 0.5 * x * (1.0 + jnp.tanh(c * (x + 0.044715 * (x * x * x))))


def _rms_scale(x):
    return x * lax.rsqrt(jnp.mean(x * x, axis=-1, keepdims=True) + RMS_EPS)


def _const_spec(shape):
    zeros = (0,) * len(shape)
    return pl.BlockSpec(shape, lambda *_: zeros)


def _us_proj_kernel(x_ref, g_ref, w_ref, u_ref, slab_ref, *, tile, nblk):
    h = _rms_scale(x_ref[...]) * g_ref[...]
    u = jnp.dot(h.astype(jnp.bfloat16), w_ref[...], preferred_element_type=jnp.float32)
    for b in range(nblk):
        slab_ref[b] = u[:, b * LANES:(b + 1) * LANES]
    for b in range(nblk):
        for k in range(CHUNK):
            u_ref[b, :, k * LANES:(k + 1) * LANES] = slab_ref[
                b, pl.ds(k, tile // CHUNK, stride=CHUNK), :].astype(u_ref.dtype)


def _us_proj(x2, pre_g, w_in, col, n):
    t, d = x2.shape
    nblk = n // LANES
    assert col % n == 0
    kern = functools.partial(_us_proj_kernel, tile=US_TILE, nblk=nblk)
    return pl.pallas_call(
        kern,
        grid=(t // US_TILE,),
        in_specs=[pl.BlockSpec((US_TILE, d), lambda i: (i, 0)),
                  _const_spec((1, d)),
                  pl.BlockSpec((d, n), lambda i: (0, col // n))],
        out_specs=pl.BlockSpec((nblk, US_TILE // CHUNK, ROW_WIDTH), lambda i: (0, i, 0)),
        out_shape=jax.ShapeDtypeStruct((nblk, t // CHUNK, ROW_WIDTH), jnp.bfloat16),
        scratch_shapes=[pltpu.VMEM((nblk, US_TILE, LANES), jnp.float32)],
        compiler_params=pltpu.CompilerParams(dimension_semantics=("arbitrary",),
                                             vmem_limit_bytes=VMEM_LIMIT),
        name="us_proj",
    )(x2, pre_g, w_in)


def _ssm_compact(lam_re, lam_im, log_dt, b_re, b_im, c_re, c_im, d, n_levels):
    groups, states = lam_re.shape
    gh = groups * SSM_GROUP
    dt = jnp.exp(log_dt)[:, None]
    lr, li = lam_re, lam_im
    scan_pows = [CHUNK << lev for lev in range(n_levels)]
    row_pows = [CHUNK * r for r in range(SUBLANES)]
    nv = jnp.asarray(list(range(CHUNK + 1)) + scan_pows + row_pows, jnp.float32)[:, None, None]
    mag = jnp.exp(nv * (lr * dt))
    pw_r = mag * jnp.cos(nv * (li * dt))
    pw_i = mag * jnp.sin(nv * (li * dt))
    pr, pi = pw_r[:CHUNK + 1], pw_i[:CHUNK + 1]
    ar, ai = pr[1], pi[1]
    den = lr * lr + li * li
    zr = ((ar - 1.0) * lr + ai * li) / den
    zi = (ai * lr - (ar - 1.0) * li) / den
    bbr = zr[..., None] * b_re - zi[..., None] * b_im
    bbi = zr[..., None] * b_im + zi[..., None] * b_re

    bt_r = bbr.transpose(2, 0, 1).reshape(1, SSM_GROUP, groups * states)
    bt_i = bbi.transpose(2, 0, 1).reshape(1, SSM_GROUP, groups * states)
    qr = pr[CHUNK - 1::-1].reshape(CHUNK, 1, groups * states)
    qi = pi[CHUNK - 1::-1].reshape(CHUNK, 1, groups * states)
    fb = jnp.stack([qr * bt_r - qi * bt_i, qr * bt_i + qi * bt_r])

    spread = lambda v: jnp.repeat(jnp.swapaxes(v, -1, -2), SSM_GROUP, axis=-1)
    px_r, px_i = spread(pr), spread(pi)
    ct_r = c_re.transpose(2, 0, 1).reshape(1, states, gh)
    ct_i = c_im.transpose(2, 0, 1).reshape(1, states, gh)
    m_r = ct_r * px_r - ct_i * px_i
    m_i = ct_r * px_i + ct_i * px_r
    mb = jnp.stack([m_r[1:], -m_i[1:]])
    bx_r = spread(bbr.transpose(2, 0, 1))
    bx_i = spread(bbi.transpose(2, 0, 1))
    kb = jnp.sum(m_r[:CHUNK, None] * bx_r[None] - m_i[:CHUNK, None] * bx_i[None], axis=2)
    skip = (d.T[:, :, None] * jnp.eye(SSM_GROUP, dtype=d.dtype)[:, None, :]).reshape(SSM_GROUP, gh)
    kb = kb.at[0].add(skip)

    nblk = groups // GROUPS_PER_BLOCK
    lo, hi = CHUNK + 1, CHUNK + 1 + n_levels
    ap = jnp.stack([pw_r[lo:hi], pw_i[lo:hi]], axis=1)
    ap = ap.reshape(2 * n_levels, nblk, BLOCK_STATES).transpose(1, 0, 2)
    aq = jnp.concatenate([pw_r[hi:], pw_i[hi:]])
    aq = aq.reshape(2 * SUBLANES, nblk, BLOCK_STATES).transpose(1, 0, 2)
    return kb, fb, mb, ap, aq


def _block_diag(compact, group_rows, group_cols):
    n = compact.shape[1]
    tiled = jnp.concatenate([compact] * GROUPS_PER_BLOCK, axis=0)
    r = lax.broadcasted_iota(jnp.int32, tiled.shape, 0) // group_rows
    c = lax.broadcasted_iota(jnp.int32, tiled.shape, 1) // group_cols
    return jnp.where(r == c, tiled, 0.0).astype(jnp.bfloat16)


def _ssm_chunk_kernel(u_ref, kb_ref, fb_ref, mb_ref, ap_ref, aq_ref, y_ref,
                      tz_ref, wb_ref, wc_ref, cre_ref, cim_ref, lre_ref, lim_ref, ebr_ref, ebi_ref,
                      *, rows, n_levels, blocks_per_seq):
    half = BLOCK_STATES

    @pl.when(pl.program_id(1) == 0)
    def _():
        tz_ref[...] = jnp.zeros_like(tz_ref)
        for lag in range(CHUNK):
            blk = _block_diag(kb_ref[lag], SSM_GROUP, SSM_GROUP)
            for k in range(CHUNK - lag):
                t = k + lag
                tz_ref[k * LANES:(k + 1) * LANES, t * LANES:(t + 1) * LANES] = blk
        for ri in range(2):
            for k in range(CHUNK):
                wb_ref[k * LANES:(k + 1) * LANES, ri * half:(ri + 1) * half] = _block_diag(
                    fb_ref[ri, k], SSM_GROUP, SSM_STATE)
                wc_ref[ri * half:(ri + 1) * half, k * LANES:(k + 1) * LANES] = _block_diag(
                    mb_ref[ri, k], SSM_STATE, SSM_GROUP)

    @pl.when(pl.program_id(1) % blocks_per_seq == 0)
    def _():
        cre_ref[...] = jnp.zeros_like(cre_ref)
        cim_ref[...] = jnp.zeros_like(cim_ref)

    def scan_level(vr, vi, lev, shift, pos):
        mr, mi = ap_ref[0, 2 * lev:2 * lev + 1, :], ap_ref[0, 2 * lev + 1:2 * lev + 2, :]
        keep = pos >= shift
        sr = jnp.where(keep, pltpu.roll(vr, shift, 0), 0.0)
        si = jnp.where(keep, pltpu.roll(vi, shift, 0), 0.0)
        return vr + (mr * sr - mi * si), vi + (mr * si + mi * sr)

    u = u_ref[0]
    y1 = jnp.dot(u, tz_ref[...], preferred_element_type=jnp.float32)
    bc = jnp.dot(u, wb_ref[...], preferred_element_type=jnp.float32)
    n_grp = rows // SUBLANES
    in_levels = SUBLANES.bit_length() - 1
    assert n_grp == 1 << (n_levels - in_levels)

    tiles = lambda v: v.reshape(n_grp, SUBLANES, half)
    lre, lim = tiles(bc[:, :half]), tiles(bc[:, half:])
    rin = lax.broadcasted_iota(jnp.int32, (1, SUBLANES, 1), 1)
    for lev in range(in_levels):
        s = 1 << lev
        mr = jnp.where(rin >= s, ap_ref[0, 2 * lev:2 * lev + 1, :][None], 0.0)
        mi = jnp.where(rin >= s, ap_ref[0, 2 * lev + 1:2 * lev + 2, :][None], 0.0)
        sr, si = pltpu.roll(lre, s, 1), pltpu.roll(lim, s, 1)
        lre, lim = lre + (mr * sr - mi * si), lim + (mr * si + mi * sr)
    lsh_re = jnp.where(rin >= 1, pltpu.roll(lre, 1, 1), 0.0).reshape(rows, half)
    lsh_im = jnp.where(rin >= 1, pltpu.roll(lim, 1, 1), 0.0).reshape(rows, half)
    lre, lim = lre.reshape(rows, half), lim.reshape(rows, half)
    nslab = half // LANES
    slabs = [slice(q * LANES, (q + 1) * LANES) for q in range(nslab)]
    for q in range(nslab):
        lre_ref[q] = lre[:, slabs[q]]
        lim_ref[q] = lim[:, slabs[q]]

    ends = pl.ds(SUBLANES - 1, n_grp, stride=SUBLANES)
    ere = jnp.concatenate([lre_ref[q, ends, :] for q in range(nslab)], axis=1)
    eim = jnp.concatenate([lim_ref[q, ends, :] for q in range(nslab)], axis=1)
    grp = lax.broadcasted_iota(jnp.int32, (n_grp, 1), 0)
    cre, cim = cre_ref[...], cim_ref[...]
    gr, gi = ap_ref[0, 2 * in_levels:2 * in_levels + 1, :], ap_ref[0, 2 * in_levels + 1:2 * in_levels + 2, :]
    ere = ere + jnp.where(grp == 0, gr * cre - gi * cim, 0.0)
    eim = eim + jnp.where(grp == 0, gr * cim + gi * cre, 0.0)
    for lev in range(in_levels, n_levels):
        ere, eim = scan_level(ere, eim, lev, 1 << (lev - in_levels), grp)
    pre = jnp.where(grp == 0, cre, pltpu.roll(ere, 1, 0))
    pim = jnp.where(grp == 0, cim, pltpu.roll(eim, 1, 0))
    cre_ref[...] = ere[n_grp - 1:n_grp, :]
    cim_ref[...] = eim[n_grp - 1:n_grp, :]

    for q in range(nslab):
        for r in range(SUBLANES):
            ebr_ref[q, pl.ds(r, n_grp, stride=SUBLANES), :] = pre[:, slabs[q]]
            ebi_ref[q, pl.ds(r, n_grp, stride=SUBLANES), :] = pim[:, slabs[q]]
    ebr = jnp.concatenate([ebr_ref[q] for q in range(nslab)], axis=1)
    ebi = jnp.concatenate([ebi_ref[q] for q in range(nslab)], axis=1)
    qr = jnp.concatenate([aq_ref[0, 0:SUBLANES, :]] * n_grp, axis=0)
    qi = jnp.concatenate([aq_ref[0, SUBLANES:2 * SUBLANES, :]] * n_grp, axis=0)
    st_re = lsh_re + (qr * ebr - qi * ebi)
    st_im = lsh_im + (qr * ebi + qi * ebr)
    sst = jnp.concatenate([st_re, st_im], axis=1).astype(jnp.bfloat16)
    y = y1 + jnp.dot(sst, wc_ref[...], preferred_element_type=jnp.float32)
    for t in range(CHUNK):
        y_ref[pl.ds(t, rows, stride=CHUNK), :] = y[:, t * LANES:(t + 1) * LANES]


def _ssm_chunk(u4, kb, fb, mb, ap, aq, rows_per_seq):
    nblk, r, size = u4.shape
    n_levels = ap.shape[1] // 2
    assert SSM_ROWS == 1 << n_levels and rows_per_seq % SSM_ROWS == 0 and size == ROW_WIDTH
    kern = functools.partial(_ssm_chunk_kernel, rows=SSM_ROWS, n_levels=n_levels,
                             blocks_per_seq=rows_per_seq // SSM_ROWS)
    mat = pltpu.VMEM((ROW_WIDTH, ROW_WIDTH), jnp.bfloat16)
    state = pltpu.VMEM((BLOCK_STATES // LANES, SSM_ROWS, LANES), jnp.float32)
    assert 2 * BLOCK_STATES == ROW_WIDTH
    return pl.pallas_call(
        kern,
        grid=(nblk, r // SSM_ROWS),
        in_specs=[pl.BlockSpec((1, SSM_ROWS, size), lambda g, i: (g, i, 0)),
                  pl.BlockSpec((CHUNK, SSM_GROUP, LANES), lambda g, i: (0, 0, g)),
                  pl.BlockSpec((2, CHUNK, SSM_GROUP, BLOCK_STATES), lambda g, i: (0, 0, 0, g)),
                  pl.BlockSpec((2, CHUNK, SSM_STATE, LANES), lambda g, i: (0, 0, 0, g)),
                  pl.BlockSpec((1,) + ap.shape[1:], lambda g, i: (g, 0, 0)),
                  pl.BlockSpec((1,) + aq.shape[1:], lambda g, i: (g, 0, 0))],
        out_specs=pl.BlockSpec((SSM_ROWS * CHUNK, LANES), lambda g, i: (i, g)),
        out_shape=jax.ShapeDtypeStruct((r * CHUNK, nblk * LANES), jnp.float32),
        scratch_shapes=[mat, mat, mat,
                        pltpu.VMEM((1, BLOCK_STATES), jnp.float32),
                        pltpu.VMEM((1, BLOCK_STATES), jnp.float32),
                        state, state, state, state],
        compiler_params=pltpu.CompilerParams(dimension_semantics=("arbitrary", "arbitrary"),
                                             vmem_limit_bytes=VMEM_LIMIT),
        name="ssm_chunk",
    )(u4, kb, fb, mb, ap, aq)


def _main_kernel(x_ref, y_ref, pre_g_ref, w_ref, cw_ref, cb_ref, lng_ref, lnb_ref, wco_ref,
                 wglu_ref, bglu_ref, wso_ref, wout_ref, post_g_ref, o_ref,
                 h_ref, glu_ref, cs_ref, cv_ref, pz_ref, so_ref, a_ref, acc_ref, yg_ref, yb_ref,
                 lin_ref, *, tile, d, ds):
    bf16, f32 = jnp.bfloat16, jnp.float32
    o_zc = 2 * d
    p_zs, p_gc, p_gs, p_end = d, d + ds, 2 * d + ds, 3 * d + ds
    nlane = d // LANES
    pieces = [slice(r, r + EW_ROWS) for r in range(0, tile, EW_ROWS)]

    @pl.when(pl.program_id(1) == 0)
    def _():
        cs_ref[:, 0:CONV_HALO, :] = jnp.zeros((nlane, CONV_HALO, LANES), f32)

    for rows in pieces:
        h_ref[rows, :] = (_rms_scale(x_ref[0, rows, :]) * pre_g_ref[...]).astype(bf16)

    glu_ref[...] = jnp.dot(h_ref[...], w_ref[:, 0:o_zc], preferred_element_type=f32)
    for rows in pieces:
        cu = glu_ref[rows, 0:d] * _sigmoid(glu_ref[rows, d:2 * d])
        for j in range(nlane):
            cs_ref[j, CONV_HALO + rows.start:CONV_HALO + rows.stop, :] = cu[:, j * LANES:(j + 1) * LANES]

    def ssm_glu():
        for rows in pieces:
            yg = _gelu_tanh(y_ref[0, rows, :])
            yg_ref[rows, :] = yg
            yb_ref[rows, :] = yg.astype(bf16)
        lin_ref[...] = jnp.dot(yb_ref[...], wglu_ref[...], preferred_element_type=f32)

    def ssm_out():
        for rows in pieces:
            gate = _sigmoid(lin_ref[rows, :] + bglu_ref[...]) * _silu(pz_ref[rows, p_zs:p_gc])
            yb_ref[rows, :] = (yg_ref[rows, :] * gate).astype(bf16)
        so_ref[...] = jnp.dot(yb_ref[...], wso_ref[...], preferred_element_type=f32)

    def conv_lane_block(j, lanes):
        for r in range(tile // CONV_ROWS):
            acc = jnp.broadcast_to(cb_ref[:, lanes], (CONV_ROWS, LANES))
            for lag in range(CONV_SIZE):
                lo = CONV_HALO + r * CONV_ROWS - lag
                tap = cw_ref[CONV_SIZE - 1 - lag:CONV_SIZE - lag, lanes]
                acc = acc + tap * cs_ref[j, lo:lo + CONV_ROWS, :]
            cv_ref[r * CONV_ROWS:(r + 1) * CONV_ROWS, lanes] = acc

    pz_ref[:, 0:p_zs] = jnp.dot(h_ref[...], w_ref[:, o_zc:o_zc + p_zs], preferred_element_type=f32)
    pz_ref[:, p_zs:p_end] = jnp.dot(h_ref[...], w_ref[:, o_zc + p_zs + ds:o_zc + p_end + ds],
                                    preferred_element_type=f32)
    ssm_glu()
    ssm_out()

    def conv_step(j, carry):
        conv_lane_block(j, pl.ds(pl.multiple_of(j * LANES, LANES), LANES))
        return carry

    lax.fori_loop(0, nlane, conv_step, 0)
    cs_ref[:, 0:CONV_HALO, :] = cs_ref[:, tile:CONV_HALO + tile, :]

    for rows in pieces:
        cv = cv_ref[rows, :]
        xc = cv - jnp.mean(cv, axis=-1, keepdims=True)
        var = jnp.mean(xc * xc, axis=-1, keepdims=True)
        ln = xc * lax.rsqrt(var + LN_EPS) * lng_ref[...] + lnb_ref[...]
        a_ref[rows, :] = (_silu(ln) * _silu(pz_ref[rows, 0:p_zs])).astype(bf16)
    acc_ref[...] = jnp.dot(a_ref[...], wco_ref[...], preferred_element_type=f32)

    for rows in pieces:
        merged = (_sigmoid(pz_ref[rows, p_gc:p_gs]) * acc_ref[rows, :]
                  + _sigmoid(pz_ref[rows, p_gs:p_end]) * so_ref[rows, :])
        a_ref[rows, :] = merged.astype(bf16)
    acc_ref[...] = jnp.dot(a_ref[...], wout_ref[...], preferred_element_type=f32)
    for rows in pieces:
        o_ref[0, rows, :] = x_ref[0, rows, :] + _rms_scale(acc_ref[rows, :]) * post_g_ref[...]


def _main(x, y, pre_g, w_in, cw, cb, lng, lnb, wco, wglu, bglu, wso, wout, post_g):
    b, l, d = x.shape
    ds = y.shape[-1]
    tile = MAIN_TILE
    assert CONV_HALO >= CONV_SIZE - 1 and CONV_HALO % SUBLANES == 0 and tile >= CONV_HALO
    kern = functools.partial(_main_kernel, tile=tile, d=d, ds=ds)
    consts = (pre_g, w_in, cw, cb, lng, lnb, wco, wglu, bglu, wso, wout, post_g)
    return pl.pallas_call(
        kern,
        grid=(b, l // tile),
        in_specs=[pl.BlockSpec((1, tile, d), lambda i, j: (i, j, 0)),
                  pl.BlockSpec((1, tile, ds), lambda i, j: (i, j, 0))]
                 + [_const_spec(a.shape) for a in consts],
        out_specs=pl.BlockSpec((1, tile, d), lambda i, j: (i, j, 0)),
        out_shape=jax.ShapeDtypeStruct((b, l, d), x.dtype),
        scratch_shapes=[pltpu.VMEM((tile, d), jnp.bfloat16),
                        pltpu.VMEM((tile, 2 * d), jnp.float32),
                        pltpu.VMEM((d // LANES, CONV_HALO + tile, LANES), jnp.float32),
                        pltpu.VMEM((tile, d), jnp.float32),
                        pltpu.VMEM((tile, 3 * d + ds), jnp.float32),
                        pltpu.VMEM((tile, d), jnp.float32),
                        pltpu.VMEM((tile, d), jnp.bfloat16),
                        pltpu.VMEM((tile, d), jnp.float32),
                        pltpu.VMEM((tile, ds), jnp.float32),
                        pltpu.VMEM((tile, ds), jnp.bfloat16),
                        pltpu.VMEM((tile, ds), jnp.float32)],
        compiler_params=pltpu.CompilerParams(dimension_semantics=("arbitrary", "arbitrary"),
                                             vmem_limit_bytes=VMEM_LIMIT),
        name="main_block",
    )(x, y, *consts)


def _layer(x, pre_g, w_in, conv_w, conv_b, ln_g, ln_b, w_conv_out, lam_re, lam_im, log_dt,
           b_re, b_im, c_re, c_im, d_skip, w_glu, b_glu, w_ssm_out, w_out, post_g):
    bsz, length, d = x.shape
    ds = w_glu.shape[0]
    bf16 = jnp.bfloat16
    o_us = 3 * d
    w_bf = w_in.astype(bf16)
    row = lambda v: v.reshape(1, -1)
    assert lam_re.shape == (ds // SSM_GROUP, SSM_STATE)

    u4 = _us_proj(x.reshape(bsz * length, d), row(pre_g), w_bf, o_us, ds)
    n_levels = SSM_ROWS.bit_length() - 1
    kb, fb, mb, ap, aq = _ssm_compact(lam_re, lam_im, log_dt, b_re, b_im, c_re, c_im, d_skip, n_levels)
    y = _ssm_chunk(u4, kb, fb, mb, ap, aq, length // CHUNK).reshape(bsz, length, ds)

    cw = jnp.concatenate([conv_w, jnp.zeros((SUBLANES - CONV_SIZE % SUBLANES, d), conv_w.dtype)], axis=0)
    return _main(x, y, row(pre_g), w_bf, cw, row(conv_b), row(ln_g), row(ln_b),
                 w_conv_out.astype(bf16), w_glu.astype(bf16), row(b_glu), w_ssm_out.astype(bf16),
                 w_out.astype(bf16), row(post_g))


def kernel(x, pre_norm_gain, w_in, conv_w, conv_b, conv_ln_gain, conv_ln_bias, w_conv_out, ssm_lambda_re, ssm_lambda_im, ssm_log_dt, ssm_b_re, ssm_b_im, ssm_c_re, ssm_c_im, ssm_d, w_ssm_glu, b_ssm_glu, w_ssm_out, w_out, post_norm_gain):
    for l in range(pre_norm_gain.shape[0]):
        x = _layer(x, pre_norm_gain[l], w_in[l], conv_w[l], conv_b[l], conv_ln_gain[l],
                   conv_ln_bias[l], w_conv_out[l], ssm_lambda_re[l], ssm_lambda_im[l],
                   ssm_log_dt[l], ssm_b_re[l], ssm_b_im[l], ssm_c_re[l], ssm_c_im[l], ssm_d[l],
                   w_ssm_glu[l], b_ssm_glu[l], w_ssm_out[l], w_out[l], post_norm_gain[l])
    return x
```

```python
import functools
import math

import jax
import jax.numpy as jnp
from jax import lax
from jax.experimental import pallas as pl
from jax.experimental.pallas import tpu as pltpu

RMS_EPS = 1e-6
LN_EPS = 1e-5
CONV_SIZE = 31
SSM_GROUP = 16
SSM_STATE = 64
CHUNK = 8
LANES = 128
SUBLANES = 8
GROUPS_PER_BLOCK = LANES // SSM_GROUP
BLOCK_STATES = GROUPS_PER_BLOCK * SSM_STATE
ROW_WIDTH = CHUNK * LANES

US_TILE = 2048
SSM_ROWS = 512
MAIN_TILE = 512
CONV_HALO = 32
CONV_ROWS = 64
MXU_COLS = 256
EW_ROWS = 16
VMEM_LIMIT = 56 * 1024 * 1024


def _sigmoid(h):
    return 0.5 * jnp.tanh(h) + 0.5


def _silu(h):
    return h * jnp.tanh(h) + h


def _gelu_tanh(x):
    c = math.sqrt(2.0 / math.pi)
    return 0.5 * x * (1.0 + jnp.tanh(c * (x + 0.044715 * (x * x * x))))


def _rms_scale(x):
    return x * lax.rsqrt(jnp.mean(x * x, axis=-1, keepdims=True) + RMS_EPS)


def _const_spec(shape):
    zeros = (0,) * len(shape)
    return pl.BlockSpec(shape, lambda *_: zeros)


def _us_proj_kernel(x_ref, g_ref, w_ref, u_ref, slab_ref, *, tile, nblk):
    h = _rms_scale(x_ref[...]) * g_ref[...]
    u = jnp.dot(h.astype(jnp.bfloat16), w_ref[...], preferred_element_type=jnp.float32)
    for b in range(nblk):
        slab_ref[b] = u[:, b * LANES:(b + 1) * LANES]
    for b in range(nblk):
        for k in range(CHUNK):
            u_ref[b, :, k * LANES:(k + 1) * LANES] = slab_ref[
                b, pl.ds(k, tile // CHUNK, stride=CHUNK), :].astype(u_ref.dtype)


def _us_proj(x2, pre_g, w_in, col, n):
    t, d = x2.shape
    nblk = n // LANES
    assert col % n == 0
    kern = functools.partial(_us_proj_kernel, tile=US_TILE, nblk=nblk)
    return pl.pallas_call(
        kern,
        grid=(t // US_TILE,),
        in_specs=[pl.BlockSpec((US_TILE, d), lambda i: (i, 0)),
                  _const_spec((1, d)),
                  pl.BlockSpec((d, n), lambda i: (0, col // n))],
        out_specs=pl.BlockSpec((nblk, US_TILE // CHUNK, ROW_WIDTH), lambda i: (0, i, 0)),
        out_shape=jax.ShapeDtypeStruct((nblk, t // CHUNK, ROW_WIDTH), jnp.bfloat16),
        scratch_shapes=[pltpu.VMEM((nblk, US_TILE, LANES), jnp.float32)],
        compiler_params=pltpu.CompilerParams(dimension_semantics=("arbitrary",),
                                             vmem_limit_bytes=VMEM_LIMIT),
        name="us_proj",
    )(x2, pre_g, w_in)


def _ssm_compact(lam_re, lam_im, log_dt, b_re, b_im, c_re, c_im, d, n_levels):
    groups, states = lam_re.shape
    gh = groups * SSM_GROUP
    dt = jnp.exp(log_dt)[:, None]
    lr, li = lam_re, lam_im
    scan_pows = [CHUNK << lev for lev in range(n_levels)]
    row_pows = [CHUNK * r for r in range(SUBLANES)]
    nv = jnp.asarray(list(range(CHUNK + 1)) + scan_pows + row_pows, jnp.float32)[:, None, None]
    mag = jnp.exp(nv * (lr * dt))
    pw_r = mag * jnp.cos(nv * (li * dt))
    pw_i = mag * jnp.sin(nv * (li * dt))
    pr, pi = pw_r[:CHUNK + 1], pw_i[:CHUNK + 1]
    ar, ai = pr[1], pi[1]
    den = lr * lr + li * li
    zr = ((ar - 1.0) * lr + ai * li) / den
    zi = (ai * lr - (ar - 1.0) * li) / den
    bbr = zr[..., None] * b_re - zi[..., None] * b_im
    bbi = zr[..., None] * b_im + zi[..., None] * b_re

    bt_r = bbr.transpose(2, 0, 1).reshape(1, SSM_GROUP, groups * states)
    bt_i = bbi.transpose(2, 0, 1).reshape(1, SSM_GROUP, groups * states)
    qr = pr[CHUNK - 1::-1].reshape(CHUNK, 1, groups * states)
    qi = pi[CHUNK - 1::-1].reshape(CHUNK, 1, groups * states)
    fb = jnp.stack([qr * bt_r - qi * bt_i, qr * bt_i + qi * bt_r])

    spread = lambda v: jnp.repeat(jnp.swapaxes(v, -1, -2), SSM_GROUP, axis=-1)
    px_r, px_i = spread(pr), spread(pi)
    ct_r = c_re.transpose(2, 0, 1).reshape(1, states, gh)
    ct_i = c_im.transpose(2, 0, 1).reshape(1, states, gh)
    m_r = ct_r * px_r - ct_i * px_i
    m_i = ct_r * px_i + ct_i * px_r
    mb = jnp.stack([m_r[1:], -m_i[1:]])
    bx_r = spread(bbr.transpose(2, 0, 1))
    bx_i = spread(bbi.transpose(2, 0, 1))
    kb = jnp.sum(m_r[:CHUNK, None] * bx_r[None] - m_i[:CHUNK, None] * bx_i[None], axis=2)
    skip = (d.T[:, :, None] * jnp.eye(SSM_GROUP, dtype=d.dtype)[:, None, :]).reshape(SSM_GROUP, gh)
    kb = kb.at[0].add(skip)

    nblk = groups // GROUPS_PER_BLOCK
    lo, hi = CHUNK + 1, CHUNK + 1 + n_levels
    ap = jnp.stack([pw_r[lo:hi], pw_i[lo:hi]], axis=1)
    ap = ap.reshape(2 * n_levels, nblk, BLOCK_STATES).transpose(1, 0, 2)
    aq = jnp.concatenate([pw_r[hi:], pw_i[hi:]])
    aq = aq.reshape(2 * SUBLANES, nblk, BLOCK_STATES).transpose(1, 0, 2)
    return kb, fb, mb, ap, aq


def _block_diag(compact, group_rows, group_cols):
    n = compact.shape[1]
    tiled = jnp.concatenate([compact] * GROUPS_PER_BLOCK, axis=0)
    r = lax.broadcasted_iota(jnp.int32, tiled.shape, 0) // group_rows
    c = lax.broadcasted_iota(jnp.int32, tiled.shape, 1) // group_cols
    return jnp.where(r == c, tiled, 0.0).astype(jnp.bfloat16)


def _ssm_chunk_kernel(u_ref, kb_ref, fb_ref, mb_ref, ap_ref, aq_ref, y_ref,
                      tz_ref, wb_ref, wc_ref, cre_ref, cim_ref, lre_ref, lim_ref, ebr_ref, ebi_ref,
                      *, rows, n_levels, blocks_per_seq):
    half = BLOCK_STATES

    @pl.when(pl.program_id(1) == 0)
    def _():
        tz_ref[...] = jnp.zeros_like(tz_ref)
        for lag in range(CHUNK):
            blk = _block_diag(kb_ref[lag], SSM_GROUP, SSM_GROUP)
            for k in range(CHUNK - lag):
                t = k + lag
                tz_ref[k * LANES:(k + 1) * LANES, t * LANES:(t + 1) * LANES] = blk
        for ri in range(2):
            for k in range(CHUNK):
                wb_ref[k * LANES:(k + 1) * LANES, ri * half:(ri + 1) * half] = _block_diag(
                    fb_ref[ri, k], SSM_GROUP, SSM_STATE)
                wc_ref[ri * half:(ri + 1) * half, k * LANES:(k + 1) * LANES] = _block_diag(
                    mb_ref[ri, k], SSM_STATE, SSM_GROUP)

    @pl.when(pl.program_id(1) % blocks_per_seq == 0)
    def _():
        cre_ref[...] = jnp.zeros_like(cre_ref)
        cim_ref[...] = jnp.zeros_like(cim_ref)

    def scan_level(vr, vi, lev, shift, pos):
        mr, mi = ap_ref[0, 2 * lev:2 * lev + 1, :], ap_ref[0, 2 * lev + 1:2 * lev + 2, :]
        keep = pos >= shift
        sr = jnp.where(keep, pltpu.roll(vr, shift, 0), 0.0)
        si = jnp.where(keep, pltpu.roll(vi, shift, 0), 0.0)
        return vr + (mr * sr - mi * si), vi + (mr * si + mi * sr)

    u = u_ref[0]
    y1 = jnp.dot(u, tz_ref[...], preferred_element_type=jnp.float32)
    bc = jnp.dot(u, wb_ref[...], preferred_element_type=jnp.float32)
    n_grp = rows // SUBLANES
    in_levels = SUBLANES.bit_length() - 1
    assert n_grp == 1 << (n_levels - in_levels)

    tiles = lambda v: v.reshape(n_grp, SUBLANES, half)
    lre, lim = tiles(bc[:, :half]), tiles(bc[:, half:])
    rin = lax.broadcasted_iota(jnp.int32, (1, SUBLANES, 1), 1)
    for lev in range(in_levels):
        s = 1 << lev
        mr = jnp.where(rin >= s, ap_ref[0, 2 * lev:2 * lev + 1, :][None], 0.0)
        mi = jnp.where(rin >= s, ap_ref[0, 2 * lev + 1:2 * lev + 2, :][None], 0.0)
        sr, si = pltpu.roll(lre, s, 1), pltpu.roll(lim, s, 1)
        lre, lim = lre + (mr * sr - mi * si), lim + (mr * si + mi * sr)
    lsh_re = jnp.where(rin >= 1, pltpu.roll(lre, 1, 1), 0.0).reshape(rows, half)
    lsh_im = jnp.where(rin >= 1, pltpu.roll(lim, 1, 1), 0.0).reshape(rows, half)
    lre, lim = lre.reshape(rows, half), lim.reshape(rows, half)
    nslab = half // LANES
    slabs = [slice(q * LANES, (q + 1) * LANES) for q in range(nslab)]
    for q in range(nslab):
        lre_ref[q] = lre[:, slabs[q]]
        lim_ref[q] = lim[:, slabs[q]]

    ends = pl.ds(SUBLANES - 1, n_grp, stride=SUBLANES)
    ere = jnp.concatenate([lre_ref[q, ends, :] for q in range(nslab)], axis=1)
    eim = jnp.concatenate([lim_ref[q, ends, :] for q in range(nslab)], axis=1)
    grp = lax.broadcasted_iota(jnp.int32, (n_grp, 1), 0)
    cre, cim = cre_ref[...], cim_ref[...]
    gr, gi = ap_ref[0, 2 * in_levels:2 * in_levels + 1, :], ap_ref[0, 2 * in_levels + 1:2 * in_levels + 2, :]
    ere = ere + jnp.where(grp == 0, gr * cre - gi * cim, 0.0)
    eim = eim + jnp.where(grp == 0, gr * cim + gi * cre, 0.0)
    for lev in range(in_levels, n_levels):
        ere, eim = scan_level(ere, eim, lev, 1 << (lev - in_levels), grp)
    pre = jnp.where(grp == 0, cre, pltpu.roll(ere, 1, 0))
    pim = jnp.where(grp == 0, cim, pltpu.roll(eim, 1, 0))
    cre_ref[...] = ere[n_grp - 1:n_grp, :]
    cim_ref[...] = eim[n_grp - 1:n_grp, :]

    for q in range(nslab):
        for r in range(SUBLANES):
            ebr_ref[q, pl.ds(r, n_grp, stride=SUBLANES), :] = pre[:, slabs[q]]
            ebi_ref[q, pl.ds(r, n_grp, stride=SUBLANES), :] = pim[:, slabs[q]]
    ebr = jnp.concatenate([ebr_ref[q] for q in range(nslab)], axis=1)
    ebi = jnp.concatenate([ebi_ref[q] for q in range(nslab)], axis=1)
    qr = jnp.concatenate([aq_ref[0, 0:SUBLANES, :]] * n_grp, axis=0)
    qi = jnp.concatenate([aq_ref[0, SUBLANES:2 * SUBLANES, :]] * n_grp, axis=0)
    st_re = lsh_re + (qr * ebr - qi * ebi)
    st_im = lsh_im + (qr * ebi + qi * ebr)
    sst = jnp.concatenate([st_re, st_im], axis=1).astype(jnp.bfloat16)
    y = y1 + jnp.dot(sst, wc_ref[...], preferred_element_type=jnp.float32)
    for t in range(CHUNK):
        y_ref[pl.ds(t, rows, stride=CHUNK), :] = y[:, t * LANES:(t + 1) * LANES]


def _ssm_chunk(u4, kb, fb, mb, ap, aq, rows_per_seq):
    nblk, r, size = u4.shape
    n_levels = ap.shape[1] // 2
    assert SSM_ROWS == 1 << n_levels and rows_per_seq % SSM_ROWS == 0 and size == ROW_WIDTH
    kern = functools.partial(_ssm_chunk_kernel, rows=SSM_ROWS, n_levels=n_levels,
                             blocks_per_seq=rows_per_seq // SSM_ROWS)
    mat = pltpu.VMEM((ROW_WIDTH, ROW_WIDTH), jnp.bfloat16)
    state = pltpu.VMEM((BLOCK_STATES // LANES, SSM_ROWS, LANES), jnp.float32)
    assert 2 * BLOCK_STATES == ROW_WIDTH
    return pl.pallas_call(
        kern,
        grid=(nblk, r // SSM_ROWS),
        in_specs=[pl.BlockSpec((1, SSM_ROWS, size), lambda g, i: (g, i, 0)),
                  pl.BlockSpec((CHUNK, SSM_GROUP, LANES), lambda g, i: (0, 0, g)),
                  pl.BlockSpec((2, CHUNK, SSM_GROUP, BLOCK_STATES), lambda g, i: (0, 0, 0, g)),
                  pl.BlockSpec((2, CHUNK, SSM_STATE, LANES), lambda g, i: (0, 0, 0, g)),
                  pl.BlockSpec((1,) + ap.shape[1:], lambda g, i: (g, 0, 0)),
                  pl.BlockSpec((1,) + aq.shape[1:], lambda g, i: (g, 0, 0))],
        out_specs=pl.BlockSpec((SSM_ROWS * CHUNK, LANES), lambda g, i: (i, g)),
        out_shape=jax.ShapeDtypeStruct((r * CHUNK, nblk * LANES), jnp.float32),
        scratch_shapes=[mat, mat, mat,
                        pltpu.VMEM((1, BLOCK_STATES), jnp.float32),
                        pltpu.VMEM((1, BLOCK_STATES), jnp.float32),
                        state, state, state, state],
        compiler_params=pltpu.CompilerParams(dimension_semantics=("arbitrary", "arbitrary"),
                                             vmem_limit_bytes=VMEM_LIMIT),
        name="ssm_chunk",
    )(u4, kb, fb, mb, ap, aq)


def _main_kernel(x_ref, y_ref, pre_g_ref, w_ref, cw_ref, cb_ref, lng_ref, lnb_ref, wco_ref,
                 wglu_ref, bglu_ref, wso_ref, wout_ref, post_g_ref, o_ref,
                 h_ref, glu_ref, cs_ref, cv_ref, pz_ref, so_ref, a_ref, acc_ref, yg_ref, yb_ref,
                 lin_ref, *, tile, d, ds):
    bf16, f32 = jnp.bfloat16, jnp.float32
    o_zc = 2 * d
    p_zs, p_gc, p_gs, p_end = d, d + ds, 2 * d + ds, 3 * d + ds
    nlane = d // LANES
    pieces = [slice(r, r + EW_ROWS) for r in range(0, tile, EW_ROWS)]

    @pl.when(pl.program_id(1) == 0)
    def _():
        cs_ref[:, 0:CONV_HALO, :] = jnp.zeros((nlane, CONV_HALO, LANES), f32)

    for rows in pieces:
        h_ref[rows, :] = (_rms_scale(x_ref[0, rows, :]) * pre_g_ref[...]).astype(bf16)

    glu_ref[...] = jnp.dot(h_ref[...], w_ref[:, 0:o_zc], preferred_element_type=f32)
    for rows in pieces:
        cu = glu_ref[rows, 0:d] * _sigmoid(glu_ref[rows, d:2 * d])
        for j in range(nlane):
            cs_ref[j, CONV_HALO + rows.start:CONV_HALO + rows.stop, :] = cu[:, j * LANES:(j + 1) * LANES]

    def ssm_glu():
        for rows in pieces:
            yg = _gelu_tanh(y_ref[0, rows, :])
            yg_ref[rows, :] = yg
            yb_ref[rows, :] = yg.astype(bf16)
        lin_ref[...] = jnp.dot(yb_ref[...], wglu_ref[...], preferred_element_type=f32)

    def ssm_out():
        for rows in pieces:
            gate = _sigmoid(lin_ref[rows, :] + bglu_ref[...]) * _silu(pz_ref[rows, p_zs:p_gc])
            yb_ref[rows, :] = (yg_ref[rows, :] * gate).astype(bf16)
        so_ref[...] = jnp.dot(yb_ref[...], wso_ref[...], preferred_element_type=f32)

    def conv_lane_block(j, lanes):
        for r in range(tile // CONV_ROWS):
            acc = jnp.broadcast_to(cb_ref[:, lanes], (CONV_ROWS, LANES))
            for lag in range(CONV_SIZE):
                lo = CONV_HALO + r * CONV_ROWS - lag
                tap = cw_ref[CONV_SIZE - 1 - lag:CONV_SIZE - lag, lanes]
                acc = acc + tap * cs_ref[j, lo:lo + CONV_ROWS, :]
            cv_ref[r * CONV_ROWS:(r + 1) * CONV_ROWS, lanes] = acc

    pz_ref[:, 0:p_zs] = jnp.dot(h_ref[...], w_ref[:, o_zc:o_zc + p_zs], preferred_element_type=f32)
    pz_ref[:, p_zs:p_end] = jnp.dot(h_ref[...], w_ref[:, o_zc + p_zs + ds:o_zc + p_end + ds],
                                    preferred_element_type=f32)
    ssm_glu()
    ssm_out()

    def conv_step(j, carry):
        conv_lane_block(j, pl.ds(pl.multiple_of(j * LANES, LANES), LANES))
        return carry

    lax.fori_loop(0, nlane, conv_step, 0)
    cs_ref[:, 0:CONV_HALO, :] = cs_ref[:, tile:CONV_HALO + tile, :]

    for rows in pieces:
        cv = cv_ref[rows, :]
        xc = cv - jnp.mean(cv, axis=-1, keepdims=True)
        var = jnp.mean(xc * xc, axis=-1, keepdims=True)
        ln = xc * lax.rsqrt(var + LN_EPS) * lng_ref[...] + lnb_ref[...]
        a_ref[rows, :] = (_silu(ln) * _silu(pz_ref[rows, 0:p_zs])).astype(bf16)
    acc_ref[...] = jnp.dot(a_ref[...], wco_ref[...], preferred_element_type=f32)

    for rows in pieces:
        merged = (_sigmoid(pz_ref[rows, p_gc:p_gs]) * acc_ref[rows, :]
                  + _sigmoid(pz_ref[rows, p_gs:p_end]) * so_ref[rows, :])
        a_ref[rows, :] = merged.astype(bf16)
    acc_ref[...] = jnp.dot(a_ref[...], wout_ref[...], preferred_element_type=f32)
    for rows in pieces:
        o_ref[0, rows, :] = x_ref[0, rows, :] + _rms_scale(acc_ref[rows, :]) * post_g_ref[...]


def _main(x, y, pre_g, w_in, cw, cb, lng, lnb, wco, wglu, bglu, wso, wout, post_g):
    b, l, d = x.shape
    ds = y.shape[-1]
    tile = MAIN_TILE
    assert CONV_HALO >= CONV_SIZE - 1 and CONV_HALO % SUBLANES == 0 and tile >= CONV_HALO
    kern = functools.partial(_main_kernel, tile=tile, d=d, ds=ds)
    consts = (pre_g, w_in, cw, cb, lng, lnb, wco, wglu, bglu, wso, wout, post_g)
    return pl.pallas_call(
        kern,
        grid=(b, l // tile),
        in_specs=[pl.BlockSpec((1, tile, d), lambda i, j: (i, j, 0)),
                  pl.BlockSpec((1, tile, ds), lambda i, j: (i, j, 0))]
                 + [_const_spec(a.shape) for a in consts],
        out_specs=pl.BlockSpec((1, tile, d), lambda i, j: (i, j, 0)),
        out_shape=jax.ShapeDtypeStruct((b, l, d), x.dtype),
        scratch_shapes=[pltpu.VMEM((tile, d), jnp.bfloat16),
                        pltpu.VMEM((tile, 2 * d), jnp.float32),
                        pltpu.VMEM((d // LANES, CONV_HALO + tile, LANES), jnp.float32),
                        pltpu.VMEM((tile, d), jnp.float32),
                        pltpu.VMEM((tile, 3 * d + ds), jnp.float32),
                        pltpu.VMEM((tile, d), jnp.float32),
                        pltpu.VMEM((tile, d), jnp.bfloat16),
                        pltpu.VMEM((tile, d), jnp.float32),
                        pltpu.VMEM((tile, ds), jnp.float32),
                        pltpu.VMEM((tile, ds), jnp.bfloat16),
                        pltpu.VMEM((tile, ds), jnp.float32)],
        compiler_params=pltpu.CompilerParams(dimension_semantics=("arbitrary", "arbitrary"),
                                             vmem_limit_bytes=VMEM_LIMIT),
        name="main_block",
    )(x, y, *consts)


def _layer(x, pre_g, w_in, conv_w, conv_b, ln_g, ln_b, w_conv_out, lam_re, lam_im, log_dt,
           b_re, b_im, c_re, c_im, d_skip, w_glu, b_glu, w_ssm_out, w_out, post_g):
    bsz, length, d = x.shape
    ds = w_glu.shape[0]
    bf16 = jnp.bfloat16
    o_us = 3 * d
    col = jnp.arange(w_in.shape[1])
    half_cols = ~((col < d) | ((col >= o_us) & (col < o_us + ds)))
    w_bf = (w_in * jnp.where(half_cols, 0.5, 1.0).astype(w_in.dtype)).astype(bf16)
    row = lambda v: v.reshape(1, -1)
    assert lam_re.shape == (ds // SSM_GROUP, SSM_STATE)

    u4 = _us_proj(x.reshape(bsz * length, d), row(pre_g), w_bf, o_us, ds)
    n_levels = SSM_ROWS.bit_length() - 1
    kb, fb, mb, ap, aq = _ssm_compact(lam_re, lam_im, log_dt, b_re, b_im, c_re, c_im, d_skip, n_levels)
    y = _ssm_chunk(u4, kb, fb, mb, ap, aq, length // CHUNK).reshape(bsz, length, ds)

    cw = jnp.concatenate([conv_w, jnp.zeros((SUBLANES - CONV_SIZE % SUBLANES, d), conv_w.dtype)], axis=0)
    return _main(x, y, row(pre_g), w_bf, cw, row(conv_b), row(0.5 * ln_g), row(0.5 * ln_b),
                 w_conv_out.astype(bf16), (0.5 * w_glu).astype(bf16), row(0.5 * b_glu),
                 w_ssm_out.astype(bf16), w_out.astype(bf16), row(post_g))


def kernel(x, pre_norm_gain, w_in, conv_w, conv_b, conv_ln_gain, conv_ln_bias, w_conv_out, ssm_lambda_re, ssm_lambda_im, ssm_log_dt, ssm_b_re, ssm_b_im, ssm_c_re, ssm_c_im, ssm_d, w_ssm_glu, b_ssm_glu, w_ssm_out, w_out, post_norm_gain):
    for l in range(pre_norm_gain.shape[0]):
        x = _layer(x, pre_norm_gain[l], w_in[l], conv_w[l], conv_b[l], conv_ln_gain[l],
                   conv_ln_bias[l], w_conv_out[l], ssm_lambda_re[l], ssm_lambda_im[l],
                   ssm_log_dt[l], ssm_b_re[l], ssm_b_im[l], ssm_c_re[l], ssm_c_im[l], ssm_d[l],
                   w_ssm_glu[l], b_ssm_glu[l], w_ssm_out[l], w_out[l], post_norm_gain[l])
    return x
```

```python
import functools
import math

import jax
import jax.numpy as jnp
from jax import lax
from jax.experimental import pallas as pl
from jax.experimental.pallas import tpu as pltpu

RMS_EPS = 1e-6
LN_EPS = 1e-5
CONV_SIZE = 31
SSM_GROUP = 16
SSM_STATE = 64
CHUNK = 8
LANES = 128
SUBLANES = 8
GROUPS_PER_BLOCK = LANES // SSM_GROUP
BLOCK_STATES = GROUPS_PER_BLOCK * SSM_STATE
ROW_WIDTH = CHUNK * LANES

US_TILE = 2048
SSM_ROWS = 512
MAIN_TILE = 512
CONV_HALO = 32
CONV_ROWS = 64
MXU_COLS = 256
EW_ROWS = 16
VMEM_LIMIT = 56 * 1024 * 1024


def _sigmoid_x2(h):
    return jnp.tanh(h) + 1.0


def _silu(h):
    return h * jnp.tanh(h) + h


def _gelu_tanh(x):
    c = math.sqrt(2.0 / math.pi)
    return 0.5 * x * (1.0 + jnp.tanh(c * (x + 0.044715 * (x * x * x))))


def _rms_scale(x):
    return x * lax.rsqrt(jnp.mean(x * x, axis=-1, keepdims=True) + RMS_EPS)


def _const_spec(shape):
    zeros = (0,) * len(shape)
    return pl.BlockSpec(shape, lambda *_: zeros)


def _us_proj_kernel(x_ref, g_ref, w_ref, u_ref, slab_ref, *, tile, nblk):
    h = _rms_scale(x_ref[...]) * g_ref[...]
    u = jnp.dot(h.astype(jnp.bfloat16), w_ref[...], preferred_element_type=jnp.float32)
    for b in range(nblk):
        slab_ref[b] = u[:, b * LANES:(b + 1) * LANES]
    for b in range(nblk):
        for k in range(CHUNK):
            u_ref[b, :, k * LANES:(k + 1) * LANES] = slab_ref[
                b, pl.ds(k, tile // CHUNK, stride=CHUNK), :].astype(u_ref.dtype)


def _us_proj(x2, pre_g, w_in, col, n):
    t, d = x2.shape
    nblk = n // LANES
    assert col % n == 0
    kern = functools.partial(_us_proj_kernel, tile=US_TILE, nblk=nblk)
    return pl.pallas_call(
        kern,
        grid=(t // US_TILE,),
        in_specs=[pl.BlockSpec((US_TILE, d), lambda i: (i, 0)),
                  _const_spec((1, d)),
                  pl.BlockSpec((d, n), lambda i: (0, col // n))],
        out_specs=pl.BlockSpec((nblk, US_TILE // CHUNK, ROW_WIDTH), lambda i: (0, i, 0)),
        out_shape=jax.ShapeDtypeStruct((nblk, t // CHUNK, ROW_WIDTH), jnp.bfloat16),
        scratch_shapes=[pltpu.VMEM((nblk, US_TILE, LANES), jnp.float32)],
        compiler_params=pltpu.CompilerParams(dimension_semantics=("arbitrary",),
                                             vmem_limit_bytes=VMEM_LIMIT),
        name="us_proj",
    )(x2, pre_g, w_in)


def _ssm_compact(lam_re, lam_im, log_dt, b_re, b_im, c_re, c_im, d, n_levels):
    groups, states = lam_re.shape
    gh = groups * SSM_GROUP
    dt = jnp.exp(log_dt)[:, None]
    lr, li = lam_re, lam_im
    scan_pows = [CHUNK << lev for lev in range(n_levels)]
    row_pows = [CHUNK * r for r in range(SUBLANES)]
    nv = jnp.asarray(list(range(CHUNK + 1)) + scan_pows + row_pows, jnp.float32)[:, None, None]
    mag = jnp.exp(nv * (lr * dt))
    pw_r = mag * jnp.cos(nv * (li * dt))
    pw_i = mag * jnp.sin(nv * (li * dt))
    pr, pi = pw_r[:CHUNK + 1], pw_i[:CHUNK + 1]
    ar, ai = pr[1], pi[1]
    den = lr * lr + li * li
    zr = ((ar - 1.0) * lr + ai * li) / den
    zi = (ai * lr - (ar - 1.0) * li) / den
    bbr = zr[..., None] * b_re - zi[..., None] * b_im
    bbi = zr[..., None] * b_im + zi[..., None] * b_re

    bt_r = bbr.transpose(2, 0, 1).reshape(1, SSM_GROUP, groups * states)
    bt_i = bbi.transpose(2, 0, 1).reshape(1, SSM_GROUP, groups * states)
    qr = pr[CHUNK - 1::-1].reshape(CHUNK, 1, groups * states)
    qi = pi[CHUNK - 1::-1].reshape(CHUNK, 1, groups * states)
    fb = jnp.stack([qr * bt_r - qi * bt_i, qr * bt_i + qi * bt_r])

    spread = lambda v: jnp.repeat(jnp.swapaxes(v, -1, -2), SSM_GROUP, axis=-1)
    px_r, px_i = spread(pr), spread(pi)
    ct_r = c_re.transpose(2, 0, 1).reshape(1, states, gh)
    ct_i = c_im.transpose(2, 0, 1).reshape(1, states, gh)
    m_r = ct_r * px_r - ct_i * px_i
    m_i = ct_r * px_i + ct_i * px_r
    mb = jnp.stack([m_r[1:], -m_i[1:]])
    bx_r = spread(bbr.transpose(2, 0, 1))
    bx_i = spread(bbi.transpose(2, 0, 1))
    kb = jnp.sum(m_r[:CHUNK, None] * bx_r[None] - m_i[:CHUNK, None] * bx_i[None], axis=2)
    skip = (d.T[:, :, None] * jnp.eye(SSM_GROUP, dtype=d.dtype)[:, None, :]).reshape(SSM_GROUP, gh)
    kb = kb.at[0].add(skip)

    nblk = groups // GROUPS_PER_BLOCK
    lo, hi = CHUNK + 1, CHUNK + 1 + n_levels
    ap = jnp.stack([pw_r[lo:hi], pw_i[lo:hi]], axis=1)
    ap = ap.reshape(2 * n_levels, nblk, BLOCK_STATES).transpose(1, 0, 2)
    aq = jnp.concatenate([pw_r[hi:], pw_i[hi:]])
    aq = aq.reshape(2 * SUBLANES, nblk, BLOCK_STATES).transpose(1, 0, 2)
    return kb, fb, mb, ap, aq


def _block_diag(compact, group_rows, group_cols):
    n = compact.shape[1]
    tiled = jnp.concatenate([compact] * GROUPS_PER_BLOCK, axis=0)
    r = lax.broadcasted_iota(jnp.int32, tiled.shape, 0) // group_rows
    c = lax.broadcasted_iota(jnp.int32, tiled.shape, 1) // group_cols
    return jnp.where(r == c, tiled, 0.0).astype(jnp.bfloat16)


def _ssm_chunk_kernel(u_ref, kb_ref, fb_ref, mb_ref, ap_ref, aq_ref, y_ref,
                      tz_ref, wb_ref, wc_ref, cre_ref, cim_ref, lre_ref, lim_ref, ebr_ref, ebi_ref,
                      *, rows, n_levels, blocks_per_seq):
    half = BLOCK_STATES

    @pl.when(pl.program_id(1) == 0)
    def _():
        tz_ref[...] = jnp.zeros_like(tz_ref)
        for lag in range(CHUNK):
            blk = _block_diag(kb_ref[lag], SSM_GROUP, SSM_GROUP)
            for k in range(CHUNK - lag):
                t = k + lag
                tz_ref[k * LANES:(k + 1) * LANES, t * LANES:(t + 1) * LANES] = blk
        for ri in range(2):
            for k in range(CHUNK):
                wb_ref[k * LANES:(k + 1) * LANES, ri * half:(ri + 1) * half] = _block_diag(
                    fb_ref[ri, k], SSM_GROUP, SSM_STATE)
                wc_ref[ri * half:(ri + 1) * half, k * LANES:(k + 1) * LANES] = _block_diag(
                    mb_ref[ri, k], SSM_STATE, SSM_GROUP)

    @pl.when(pl.program_id(1) % blocks_per_seq == 0)
    def _():
        cre_ref[...] = jnp.zeros_like(cre_ref)
        cim_ref[...] = jnp.zeros_like(cim_ref)

    def scan_level(vr, vi, lev, shift, pos):
        mr, mi = ap_ref[0, 2 * lev:2 * lev + 1, :], ap_ref[0, 2 * lev + 1:2 * lev + 2, :]
        keep = pos >= shift
        sr = jnp.where(keep, pltpu.roll(vr, shift, 0), 0.0)
        si = jnp.where(keep, pltpu.roll(vi, shift, 0), 0.0)
        return vr + (mr * sr - mi * si), vi + (mr * si + mi * sr)

    u = u_ref[0]
    y1 = jnp.dot(u, tz_ref[...], preferred_element_type=jnp.float32)
    bc = jnp.dot(u, wb_ref[...], preferred_element_type=jnp.float32)
    n_grp = rows // SUBLANES
    in_levels = SUBLANES.bit_length() - 1
    assert n_grp == 1 << (n_levels - in_levels)

    tiles = lambda v: v.reshape(n_grp, SUBLANES, half)
    lre, lim = tiles(bc[:, :half]), tiles(bc[:, half:])
    rin = lax.broadcasted_iota(jnp.int32, (1, SUBLANES, 1), 1)
    for lev in range(in_levels):
        s = 1 << lev
        mr = jnp.where(rin >= s, ap_ref[0, 2 * lev:2 * lev + 1, :][None], 0.0)
        mi = jnp.where(rin >= s, ap_ref[0, 2 * lev + 1:2 * lev + 2, :][None], 0.0)
        sr, si = pltpu.roll(lre, s, 1), pltpu.roll(lim, s, 1)
        lre, lim = lre + (mr * sr - mi * si), lim + (mr * si + mi * sr)
    lsh_re = jnp.where(rin >= 1, pltpu.roll(lre, 1, 1), 0.0).reshape(rows, half)
    lsh_im = jnp.where(rin >= 1, pltpu.roll(lim, 1, 1), 0.0).reshape(rows, half)
    lre, lim = lre.reshape(rows, half), lim.reshape(rows, half)
    nslab = half // LANES
    slabs = [slice(q * LANES, (q + 1) * LANES) for q in range(nslab)]
    for q in range(nslab):
        lre_ref[q] = lre[:, slabs[q]]
        lim_ref[q] = lim[:, slabs[q]]

    ends = pl.ds(SUBLANES - 1, n_grp, stride=SUBLANES)
    ere = jnp.concatenate([lre_ref[q, ends, :] for q in range(nslab)], axis=1)
    eim = jnp.concatenate([lim_ref[q, ends, :] for q in range(nslab)], axis=1)
    grp = lax.broadcasted_iota(jnp.int32, (n_grp, 1), 0)
    cre, cim = cre_ref[...], cim_ref[...]
    gr, gi = ap_ref[0, 2 * in_levels:2 * in_levels + 1, :], ap_ref[0, 2 * in_levels + 1:2 * in_levels + 2, :]
    ere = ere + jnp.where(grp == 0, gr * cre - gi * cim, 0.0)
    eim = eim + jnp.where(grp == 0, gr * cim + gi * cre, 0.0)
    for lev in range(in_levels, n_levels):
        ere, eim = scan_level(ere, eim, lev, 1 << (lev - in_levels), grp)
    pre = jnp.where(grp == 0, cre, pltpu.roll(ere, 1, 0))
    pim = jnp.where(grp == 0, cim, pltpu.roll(eim, 1, 0))
    cre_ref[...] = ere[n_grp - 1:n_grp, :]
    cim_ref[...] = eim[n_grp - 1:n_grp, :]

    for q in range(nslab):
        for r in range(SUBLANES):
            ebr_ref[q, pl.ds(r, n_grp, stride=SUBLANES), :] = pre[:, slabs[q]]
            ebi_ref[q, pl.ds(r, n_grp, stride=SUBLANES), :] = pim[:, slabs[q]]
    ebr = jnp.concatenate([ebr_ref[q] for q in range(nslab)], axis=1)
    ebi = jnp.concatenate([ebi_ref[q] for q in range(nslab)], axis=1)
    qr = jnp.concatenate([aq_ref[0, 0:SUBLANES, :]] * n_grp, axis=0)
    qi = jnp.concatenate([aq_ref[0, SUBLANES:2 * SUBLANES, :]] * n_grp, axis=0)
    st_re = lsh_re + (qr * ebr - qi * ebi)
    st_im = lsh_im + (qr * ebi + qi * ebr)
    sst = jnp.concatenate([st_re, st_im], axis=1).astype(jnp.bfloat16)
    y = y1 + jnp.dot(sst, wc_ref[...], preferred_element_type=jnp.float32)
    for t in range(CHUNK):
        y_ref[pl.ds(t, rows, stride=CHUNK), :] = y[:, t * LANES:(t + 1) * LANES]


def _ssm_chunk(u4, kb, fb, mb, ap, aq, rows_per_seq):
    nblk, r, size = u4.shape
    n_levels = ap.shape[1] // 2
    assert SSM_ROWS == 1 << n_levels and rows_per_seq % SSM_ROWS == 0 and size == ROW_WIDTH
    kern = functools.partial(_ssm_chunk_kernel, rows=SSM_ROWS, n_levels=n_levels,
                             blocks_per_seq=rows_per_seq // SSM_ROWS)
    mat = pltpu.VMEM((ROW_WIDTH, ROW_WIDTH), jnp.bfloat16)
    state = pltpu.VMEM((BLOCK_STATES // LANES, SSM_ROWS, LANES), jnp.float32)
    assert 2 * BLOCK_STATES == ROW_WIDTH
    return pl.pallas_call(
        kern,
        grid=(nblk, r // SSM_ROWS),
        in_specs=[pl.BlockSpec((1, SSM_ROWS, size), lambda g, i: (g, i, 0)),
                  pl.BlockSpec((CHUNK, SSM_GROUP, LANES), lambda g, i: (0, 0, g)),
                  pl.BlockSpec((2, CHUNK, SSM_GROUP, BLOCK_STATES), lambda g, i: (0, 0, 0, g)),
                  pl.BlockSpec((2, CHUNK, SSM_STATE, LANES), lambda g, i: (0, 0, 0, g)),
                  pl.BlockSpec((1,) + ap.shape[1:], lambda g, i: (g, 0, 0)),
                  pl.BlockSpec((1,) + aq.shape[1:], lambda g, i: (g, 0, 0))],
        out_specs=pl.BlockSpec((SSM_ROWS * CHUNK, LANES), lambda g, i: (i, g)),
        out_shape=jax.ShapeDtypeStruct((r * CHUNK, nblk * LANES), jnp.float32),
        scratch_shapes=[mat, mat, mat,
                        pltpu.VMEM((1, BLOCK_STATES), jnp.float32),
                        pltpu.VMEM((1, BLOCK_STATES), jnp.float32),
                        state, state, state, state],
        compiler_params=pltpu.CompilerParams(dimension_semantics=("arbitrary", "arbitrary"),
                                             vmem_limit_bytes=VMEM_LIMIT),
        name="ssm_chunk",
    )(u4, kb, fb, mb, ap, aq)


def _main_kernel(x_ref, y_ref, pre_g_ref, w_ref, cw_ref, cb_ref, lng_ref, lnb_ref, wco_ref,
                 wglu_ref, bglu_ref, wso_ref, wout_ref, post_g_ref, o_ref,
                 h_ref, glu_ref, cs_ref, cv_ref, pz_ref, so_ref, a_ref, acc_ref, yg_ref, yb_ref,
                 lin_ref, *, tile, d, ds):
    bf16, f32 = jnp.bfloat16, jnp.float32
    o_zc = 2 * d
    p_zs, p_gc, p_gs, p_end = d, d + ds, 2 * d + ds, 3 * d + ds
    nlane = d // LANES
    pieces = [slice(r, r + EW_ROWS) for r in range(0, tile, EW_ROWS)]

    @pl.when(pl.program_id(1) == 0)
    def _():
        cs_ref[:, 0:CONV_HALO, :] = jnp.zeros((nlane, CONV_HALO, LANES), f32)

    for rows in pieces:
        h_ref[rows, :] = (_rms_scale(x_ref[0, rows, :]) * pre_g_ref[...]).astype(bf16)

    glu_ref[...] = jnp.dot(h_ref[...], w_ref[:, 0:o_zc], preferred_element_type=f32)
    for rows in pieces:
        cu = glu_ref[rows, 0:d] * _sigmoid_x2(glu_ref[rows, d:2 * d])
        for j in range(nlane):
            cs_ref[j, CONV_HALO + rows.start:CONV_HALO + rows.stop, :] = cu[:, j * LANES:(j + 1) * LANES]

    def ssm_glu():
        for rows in pieces:
            yg = _gelu_tanh(y_ref[0, rows, :])
            yg_ref[rows, :] = yg
            yb_ref[rows, :] = yg.astype(bf16)
        lin_ref[...] = jnp.dot(yb_ref[...], wglu_ref[...], preferred_element_type=f32)

    def ssm_out():
        for rows in pieces:
            gate = _sigmoid_x2(lin_ref[rows, :] + bglu_ref[...]) * _silu(pz_ref[rows, p_zs:p_gc])
            yb_ref[rows, :] = (yg_ref[rows, :] * gate).astype(bf16)
        so_ref[...] = jnp.dot(yb_ref[...], wso_ref[...], preferred_element_type=f32)

    def conv_lane_block(j, lanes):
        for r in range(tile // CONV_ROWS):
            acc = jnp.broadcast_to(cb_ref[:, lanes], (CONV_ROWS, LANES))
            for lag in range(CONV_SIZE):
                lo = CONV_HALO + r * CONV_ROWS - lag
                tap = cw_ref[CONV_SIZE - 1 - lag:CONV_SIZE - lag, lanes]
                acc = acc + tap * cs_ref[j, lo:lo + CONV_ROWS, :]
            cv_ref[r * CONV_ROWS:(r + 1) * CONV_ROWS, lanes] = acc

    pz_ref[:, 0:p_zs] = jnp.dot(h_ref[...], w_ref[:, o_zc:o_zc + p_zs], preferred_element_type=f32)
    pz_ref[:, p_zs:p_end] = jnp.dot(h_ref[...], w_ref[:, o_zc + p_zs + ds:o_zc + p_end + ds],
                                    preferred_element_type=f32)
    ssm_glu()
    ssm_out()

    def conv_step(j, carry):
        conv_lane_block(j, pl.ds(pl.multiple_of(j * LANES, LANES), LANES))
        return carry

    lax.fori_loop(0, nlane, conv_step, 0)
    cs_ref[:, 0:CONV_HALO, :] = cs_ref[:, tile:CONV_HALO + tile, :]

    for rows in pieces:
        cv = cv_ref[rows, :]
        xc = cv - jnp.mean(cv, axis=-1, keepdims=True)
        var = jnp.mean(xc * xc, axis=-1, keepdims=True)
        ln = xc * lax.rsqrt(var + LN_EPS) * lng_ref[...] + lnb_ref[...]
        a_ref[rows, :] = (_silu(ln) * _silu(pz_ref[rows, 0:p_zs])).astype(bf16)
    acc_ref[...] = jnp.dot(a_ref[...], wco_ref[...], preferred_element_type=f32)

    for rows in pieces:
        merged = (_sigmoid_x2(pz_ref[rows, p_gc:p_gs]) * acc_ref[rows, :]
                  + _sigmoid_x2(pz_ref[rows, p_gs:p_end]) * so_ref[rows, :])
        a_ref[rows, :] = merged.astype(bf16)
    acc_ref[...] = jnp.dot(a_ref[...], wout_ref[...], preferred_element_type=f32)
    for rows in pieces:
        o_ref[0, rows, :] = x_ref[0, rows, :] + _rms_scale(acc_ref[rows, :]) * post_g_ref[...]


def _main(x, y, pre_g, w_in, cw, cb, lng, lnb, wco, wglu, bglu, wso, wout, post_g):
    b, l, d = x.shape
    ds = y.shape[-1]
    tile = MAIN_TILE
    assert CONV_HALO >= CONV_SIZE - 1 and CONV_HALO % SUBLANES == 0 and tile >= CONV_HALO
    kern = functools.partial(_main_kernel, tile=tile, d=d, ds=ds)
    consts = (pre_g, w_in, cw, cb, lng, lnb, wco, wglu, bglu, wso, wout, post_g)
    return pl.pallas_call(
        kern,
        grid=(b, l // tile),
        in_specs=[pl.BlockSpec((1, tile, d), lambda i, j: (i, j, 0)),
                  pl.BlockSpec((1, tile, ds), lambda i, j: (i, j, 0))]
                 + [_const_spec(a.shape) for a in consts],
        out_specs=pl.BlockSpec((1, tile, d), lambda i, j: (i, j, 0)),
        out_shape=jax.ShapeDtypeStruct((b, l, d), x.dtype),
        scratch_shapes=[pltpu.VMEM((tile, d), jnp.bfloat16),
                        pltpu.VMEM((tile, 2 * d), jnp.float32),
                        pltpu.VMEM((d // LANES, CONV_HALO + tile, LANES), jnp.float32),
                        pltpu.VMEM((tile, d), jnp.float32),
                        pltpu.VMEM((tile, 3 * d + ds), jnp.float32),
                        pltpu.VMEM((tile, d), jnp.float32),
                        pltpu.VMEM((tile, d), jnp.bfloat16),
                        pltpu.VMEM((tile, d), jnp.float32),
                        pltpu.VMEM((tile, ds), jnp.float32),
                        pltpu.VMEM((tile, ds), jnp.bfloat16),
                        pltpu.VMEM((tile, ds), jnp.float32)],
        compiler_params=pltpu.CompilerParams(dimension_semantics=("arbitrary", "arbitrary"),
                                             vmem_limit_bytes=VMEM_LIMIT),
        name="main_block",
    )(x, y, *consts)


def _layer(x, pre_g, w_in, conv_w, conv_b, ln_g, ln_b, w_conv_out, lam_re, lam_im, log_dt,
           b_re, b_im, c_re, c_im, d_skip, w_glu, b_glu, w_ssm_out, w_out, post_g):
    bsz, length, d = x.shape
    ds = w_glu.shape[0]
    bf16 = jnp.bfloat16
    o_us = 3 * d
    col = jnp.arange(w_in.shape[1])
    w_bf = (w_in * jnp.where((col >= o_us) & (col < o_us + ds), 1.0, 0.5).astype(w_in.dtype)).astype(bf16)
    row = lambda v: v.reshape(1, -1)
    assert lam_re.shape == (ds // SSM_GROUP, SSM_STATE)

    u4 = _us_proj(x.reshape(bsz * length, d), row(pre_g), w_bf, o_us, ds)
    n_levels = SSM_ROWS.bit_length() - 1
    kb, fb, mb, ap, aq = _ssm_compact(lam_re, lam_im, log_dt, b_re, b_im, c_re, c_im, d_skip, n_levels)
    y = _ssm_chunk(u4, kb, fb, mb, ap, aq, length // CHUNK).reshape(bsz, length, ds)

    cw = jnp.concatenate([conv_w, jnp.zeros((SUBLANES - CONV_SIZE % SUBLANES, d), conv_w.dtype)], axis=0)
    return _main(x, y, row(pre_g), w_bf, cw, row(conv_b), row(0.5 * ln_g), row(0.5 * ln_b),
                 (0.5 * w_conv_out).astype(bf16), (0.5 * w_glu).astype(bf16), row(0.5 * b_glu),
                 (0.25 * w_ssm_out).astype(bf16), w_out.astype(bf16), row(post_g))


def kernel(x, pre_norm_gain, w_in, conv_w, conv_b, conv_ln_gain, conv_ln_bias, w_conv_out, ssm_lambda_re, ssm_lambda_im, ssm_log_dt, ssm_b_re, ssm_b_im, ssm_c_re, ssm_c_im, ssm_d, w_ssm_glu, b_ssm_glu, w_ssm_out, w_out, post_norm_gain):
    for l in range(pre_norm_gain.shape[0]):
        x = _layer(x, pre_norm_gain[l], w_in[l], conv_w[l], conv_b[l], conv_ln_gain[l],
                   conv_ln_bias[l], w_conv_out[l], ssm_lambda_re[l], ssm_lambda_im[l],
                   ssm_log_dt[l], ssm_b_re[l], ssm_b_im[l], ssm_c_re[l], ssm_c_im[l], ssm_d[l],
                   w_ssm_glu[l], b_ssm_glu[l], w_ssm_out[l], w_out[l], post_norm_gain[l])
    return x
```

```python
import functools
import math

import jax
import jax.numpy as jnp
from jax import lax
from jax.experimental import pallas as pl
from jax.experimental.pallas import tpu as pltpu

RMS_EPS = 1e-6
LN_EPS = 1e-5
CONV_SIZE = 31
SSM_GROUP = 16
SSM_STATE = 64
CHUNK = 8
LANES = 128
SUBLANES = 8
GROUPS_PER_BLOCK = LANES // SSM_GROUP
BLOCK_STATES = GROUPS_PER_BLOCK * SSM_STATE
ROW_WIDTH = CHUNK * LANES

US_TILE = 2048
SSM_ROWS = 512
MAIN_TILE = 512
CONV_HALO = 32
CONV_ROWS = 64
MXU_COLS = 256
EW_ROWS = 16
VMEM_LIMIT = 56 * 1024 * 1024


def _sigmoid(h):
    return 0.5 * jnp.tanh(h) + 0.5


def _silu(h):
    return h * jnp.tanh(h) + h


def _gelu_tanh(x):
    c = math.sqrt(2.0 / math.pi)
    return 0.5 * x * (1.0 + jnp.tanh(c * (x + 0.044715 * (x * x * x))))


def _rms_scale(x):
    return x * lax.rsqrt(jnp.mean(x * x, axis=-1, keepdims=True) + RMS_EPS)


def _const_spec(shape):
    zeros = (0,) * len(shape)
    return pl.BlockSpec(shape, lambda *_: zeros)


def _us_proj_kernel(x_ref, g_ref, w_ref, u_ref, slab_ref, *, tile, nblk):
    h = _rms_scale(x_ref[...]) * g_ref[...]
    u = jnp.dot(h.astype(jnp.bfloat16), w_ref[...].astype(jnp.bfloat16),
                preferred_element_type=jnp.float32)
    for b in range(nblk):
        slab_ref[b] = u[:, b * LANES:(b + 1) * LANES]
    for b in range(nblk):
        for k in range(CHUNK):
            u_ref[b, :, k * LANES:(k + 1) * LANES] = slab_ref[
                b, pl.ds(k, tile // CHUNK, stride=CHUNK), :].astype(u_ref.dtype)


def _us_proj(x2, pre_g, w_in, col, n):
    t, d = x2.shape
    nblk = n // LANES
    assert col % n == 0
    kern = functools.partial(_us_proj_kernel, tile=US_TILE, nblk=nblk)
    return pl.pallas_call(
        kern,
        grid=(t // US_TILE,),
        in_specs=[pl.BlockSpec((US_TILE, d), lambda i: (i, 0)),
                  _const_spec((1, d)),
                  pl.BlockSpec((d, n), lambda i: (0, col // n))],
        out_specs=pl.BlockSpec((nblk, US_TILE // CHUNK, ROW_WIDTH), lambda i: (0, i, 0)),
        out_shape=jax.ShapeDtypeStruct((nblk, t // CHUNK, ROW_WIDTH), jnp.bfloat16),
        scratch_shapes=[pltpu.VMEM((nblk, US_TILE, LANES), jnp.float32)],
        compiler_params=pltpu.CompilerParams(dimension_semantics=("arbitrary",),
                                             vmem_limit_bytes=VMEM_LIMIT),
        name="us_proj",
    )(x2, pre_g, w_in)


def _ssm_compact(lam_re, lam_im, log_dt, b_re, b_im, c_re, c_im, d, n_levels):
    groups, states = lam_re.shape
    gh = groups * SSM_GROUP
    dt = jnp.exp(log_dt)[:, None]
    lr, li = lam_re, lam_im
    scan_pows = [CHUNK << lev for lev in range(n_levels)]
    row_pows = [CHUNK * r for r in range(SUBLANES)]
    nv = jnp.asarray(list(range(CHUNK + 1)) + scan_pows + row_pows, jnp.float32)[:, None, None]
    mag = jnp.exp(nv * (lr * dt))
    pw_r = mag * jnp.cos(nv * (li * dt))
    pw_i = mag * jnp.sin(nv * (li * dt))
    pr, pi = pw_r[:CHUNK + 1], pw_i[:CHUNK + 1]
    ar, ai = pr[1], pi[1]
    den = lr * lr + li * li
    zr = ((ar - 1.0) * lr + ai * li) / den
    zi = (ai * lr - (ar - 1.0) * li) / den
    bbr = zr[..., None] * b_re - zi[..., None] * b_im
    bbi = zr[..., None] * b_im + zi[..., None] * b_re

    bt_r = bbr.transpose(2, 0, 1).reshape(1, SSM_GROUP, groups * states)
    bt_i = bbi.transpose(2, 0, 1).reshape(1, SSM_GROUP, groups * states)
    qr = pr[CHUNK - 1::-1].reshape(CHUNK, 1, groups * states)
    qi = pi[CHUNK - 1::-1].reshape(CHUNK, 1, groups * states)
    fb = jnp.stack([qr * bt_r - qi * bt_i, qr * bt_i + qi * bt_r])

    spread = lambda v: jnp.repeat(jnp.swapaxes(v, -1, -2), SSM_GROUP, axis=-1)
    px_r, px_i = spread(pr), spread(pi)
    ct_r = c_re.transpose(2, 0, 1).reshape(1, states, gh)
    ct_i = c_im.transpose(2, 0, 1).reshape(1, states, gh)
    m_r = ct_r * px_r - ct_i * px_i
    m_i = ct_r * px_i + ct_i * px_r
    mb = jnp.stack([m_r[1:], -m_i[1:]])
    bx_r = spread(bbr.transpose(2, 0, 1))
    bx_i = spread(bbi.transpose(2, 0, 1))
    kb = jnp.sum(m_r[:CHUNK, None] * bx_r[None] - m_i[:CHUNK, None] * bx_i[None], axis=2)
    skip = (d.T[:, :, None] * jnp.eye(SSM_GROUP, dtype=d.dtype)[:, None, :]).reshape(SSM_GROUP, gh)
    kb = kb.at[0].add(skip)

    nblk = groups // GROUPS_PER_BLOCK
    lo, hi = CHUNK + 1, CHUNK + 1 + n_levels
    ap = jnp.stack([pw_r[lo:hi], pw_i[lo:hi]], axis=1)
    ap = ap.reshape(2 * n_levels, nblk, BLOCK_STATES).transpose(1, 0, 2)
    aq = jnp.concatenate([pw_r[hi:], pw_i[hi:]])
    aq = aq.reshape(2 * SUBLANES, nblk, BLOCK_STATES).transpose(1, 0, 2)
    return kb, fb, mb, ap, aq


def _block_diag(compact, group_rows, group_cols):
    n = compact.shape[1]
    tiled = jnp.concatenate([compact] * GROUPS_PER_BLOCK, axis=0)
    r = lax.broadcasted_iota(jnp.int32, tiled.shape, 0) // group_rows
    c = lax.broadcasted_iota(jnp.int32, tiled.shape, 1) // group_cols
    return jnp.where(r == c, tiled, 0.0).astype(jnp.bfloat16)


def _ssm_chunk_kernel(u_ref, kb_ref, fb_ref, mb_ref, ap_ref, aq_ref, wf_ref, ws_ref, y_ref, wbf_ref,
                      tz_ref, wb_ref, wc_ref, cre_ref, cim_ref, lre_ref, lim_ref, ebr_ref, ebi_ref,
                      *, rows, n_levels, blocks_per_seq):
    half = BLOCK_STATES
    wbf_ref[...] = (wf_ref[...] * ws_ref[...]).astype(wbf_ref.dtype)

    @pl.when(pl.program_id(1) == 0)
    def _():
        tz_ref[...] = jnp.zeros_like(tz_ref)
        for lag in range(CHUNK):
            blk = _block_diag(kb_ref[lag], SSM_GROUP, SSM_GROUP)
            for k in range(CHUNK - lag):
                t = k + lag
                tz_ref[k * LANES:(k + 1) * LANES, t * LANES:(t + 1) * LANES] = blk
        for ri in range(2):
            for k in range(CHUNK):
                wb_ref[k * LANES:(k + 1) * LANES, ri * half:(ri + 1) * half] = _block_diag(
                    fb_ref[ri, k], SSM_GROUP, SSM_STATE)
                wc_ref[ri * half:(ri + 1) * half, k * LANES:(k + 1) * LANES] = _block_diag(
                    mb_ref[ri, k], SSM_STATE, SSM_GROUP)

    @pl.when(pl.program_id(1) % blocks_per_seq == 0)
    def _():
        cre_ref[...] = jnp.zeros_like(cre_ref)
        cim_ref[...] = jnp.zeros_like(cim_ref)

    def scan_level(vr, vi, lev, shift, pos):
        mr, mi = ap_ref[0, 2 * lev:2 * lev + 1, :], ap_ref[0, 2 * lev + 1:2 * lev + 2, :]
        keep = pos >= shift
        sr = jnp.where(keep, pltpu.roll(vr, shift, 0), 0.0)
        si = jnp.where(keep, pltpu.roll(vi, shift, 0), 0.0)
        return vr + (mr * sr - mi * si), vi + (mr * si + mi * sr)

    u = u_ref[0]
    y1 = jnp.dot(u, tz_ref[...], preferred_element_type=jnp.float32)
    bc = jnp.dot(u, wb_ref[...], preferred_element_type=jnp.float32)
    n_grp = rows // SUBLANES
    in_levels = SUBLANES.bit_length() - 1
    assert n_grp == 1 << (n_levels - in_levels)

    tiles = lambda v: v.reshape(n_grp, SUBLANES, half)
    lre, lim = tiles(bc[:, :half]), tiles(bc[:, half:])
    rin = lax.broadcasted_iota(jnp.int32, (1, SUBLANES, 1), 1)
    for lev in range(in_levels):
        s = 1 << lev
        mr = jnp.where(rin >= s, ap_ref[0, 2 * lev:2 * lev + 1, :][None], 0.0)
        mi = jnp.where(rin >= s, ap_ref[0, 2 * lev + 1:2 * lev + 2, :][None], 0.0)
        sr, si = pltpu.roll(lre, s, 1), pltpu.roll(lim, s, 1)
        lre, lim = lre + (mr * sr - mi * si), lim + (mr * si + mi * sr)
    lsh_re = jnp.where(rin >= 1, pltpu.roll(lre, 1, 1), 0.0).reshape(rows, half)
    lsh_im = jnp.where(rin >= 1, pltpu.roll(lim, 1, 1), 0.0).reshape(rows, half)
    lre, lim = lre.reshape(rows, half), lim.reshape(rows, half)
    nslab = half // LANES
    slabs = [slice(q * LANES, (q + 1) * LANES) for q in range(nslab)]
    for q in range(nslab):
        lre_ref[q] = lre[:, slabs[q]]
        lim_ref[q] = lim[:, slabs[q]]

    ends = pl.ds(SUBLANES - 1, n_grp, stride=SUBLANES)
    ere = jnp.concatenate([lre_ref[q, ends, :] for q in range(nslab)], axis=1)
    eim = jnp.concatenate([lim_ref[q, ends, :] for q in range(nslab)], axis=1)
    grp = lax.broadcasted_iota(jnp.int32, (n_grp, 1), 0)
    cre, cim = cre_ref[...], cim_ref[...]
    gr, gi = ap_ref[0, 2 * in_levels:2 * in_levels + 1, :], ap_ref[0, 2 * in_levels + 1:2 * in_levels + 2, :]
    ere = ere + jnp.where(grp == 0, gr * cre - gi * cim, 0.0)
    eim = eim + jnp.where(grp == 0, gr * cim + gi * cre, 0.0)
    for lev in range(in_levels, n_levels):
        ere, eim = scan_level(ere, eim, lev, 1 << (lev - in_levels), grp)
    pre = jnp.where(grp == 0, cre, pltpu.roll(ere, 1, 0))
    pim = jnp.where(grp == 0, cim, pltpu.roll(eim, 1, 0))
    cre_ref[...] = ere[n_grp - 1:n_grp, :]
    cim_ref[...] = eim[n_grp - 1:n_grp, :]

    for q in range(nslab):
        for r in range(SUBLANES):
            ebr_ref[q, pl.ds(r, n_grp, stride=SUBLANES), :] = pre[:, slabs[q]]
            ebi_ref[q, pl.ds(r, n_grp, stride=SUBLANES), :] = pim[:, slabs[q]]
    ebr = jnp.concatenate([ebr_ref[q] for q in range(nslab)], axis=1)
    ebi = jnp.concatenate([ebi_ref[q] for q in range(nslab)], axis=1)
    qr = jnp.concatenate([aq_ref[0, 0:SUBLANES, :]] * n_grp, axis=0)
    qi = jnp.concatenate([aq_ref[0, SUBLANES:2 * SUBLANES, :]] * n_grp, axis=0)
    st_re = lsh_re + (qr * ebr - qi * ebi)
    st_im = lsh_im + (qr * ebi + qi * ebr)
    sst = jnp.concatenate([st_re, st_im], axis=1).astype(jnp.bfloat16)
    y = y1 + jnp.dot(sst, wc_ref[...], preferred_element_type=jnp.float32)
    for t in range(CHUNK):
        y_ref[pl.ds(t, rows, stride=CHUNK), :] = y[:, t * LANES:(t + 1) * LANES]


def _ssm_chunk(u4, kb, fb, mb, ap, aq, w_f32, w_scale, rows_per_seq):
    nblk, r, size = u4.shape
    n_levels = ap.shape[1] // 2
    assert SSM_ROWS == 1 << n_levels and rows_per_seq % SSM_ROWS == 0 and size == ROW_WIDTH
    nb = r // SSM_ROWS
    wrows, wcols = w_f32.shape
    wblk = wcols // (nblk * nb)
    assert wblk * nblk * nb == wcols and wblk % LANES == 0
    kern = functools.partial(_ssm_chunk_kernel, rows=SSM_ROWS, n_levels=n_levels,
                             blocks_per_seq=rows_per_seq // SSM_ROWS)
    mat = pltpu.VMEM((ROW_WIDTH, ROW_WIDTH), jnp.bfloat16)
    state = pltpu.VMEM((BLOCK_STATES // LANES, SSM_ROWS, LANES), jnp.float32)
    assert 2 * BLOCK_STATES == ROW_WIDTH
    return pl.pallas_call(
        kern,
        grid=(nblk, r // SSM_ROWS),
        in_specs=[pl.BlockSpec((1, SSM_ROWS, size), lambda g, i: (g, i, 0)),
                  pl.BlockSpec((CHUNK, SSM_GROUP, LANES), lambda g, i: (0, 0, g)),
                  pl.BlockSpec((2, CHUNK, SSM_GROUP, BLOCK_STATES), lambda g, i: (0, 0, 0, g)),
                  pl.BlockSpec((2, CHUNK, SSM_STATE, LANES), lambda g, i: (0, 0, 0, g)),
                  pl.BlockSpec((1,) + ap.shape[1:], lambda g, i: (g, 0, 0)),
                  pl.BlockSpec((1,) + aq.shape[1:], lambda g, i: (g, 0, 0)),
                  pl.BlockSpec((wrows, wblk), lambda g, i: (0, g * nb + i)),
                  pl.BlockSpec((1, wblk), lambda g, i: (0, g * nb + i))],
        out_specs=[pl.BlockSpec((SSM_ROWS * CHUNK, LANES), lambda g, i: (i, g)),
                   pl.BlockSpec((wrows, wblk), lambda g, i: (0, g * nb + i))],
        out_shape=[jax.ShapeDtypeStruct((r * CHUNK, nblk * LANES), jnp.float32),
                   jax.ShapeDtypeStruct((wrows, wcols), jnp.bfloat16)],
        scratch_shapes=[mat, mat, mat,
                        pltpu.VMEM((1, BLOCK_STATES), jnp.float32),
                        pltpu.VMEM((1, BLOCK_STATES), jnp.float32),
                        state, state, state, state],
        compiler_params=pltpu.CompilerParams(dimension_semantics=("arbitrary", "arbitrary"),
                                             vmem_limit_bytes=VMEM_LIMIT),
        name="ssm_chunk",
    )(u4, kb, fb, mb, ap, aq, w_f32, w_scale)


def _main_kernel(x_ref, y_ref, pre_g_ref, w_ref, cw_ref, cb_ref, lng_ref, lnb_ref, wco_ref,
                 wglu_ref, bglu_ref, wso_ref, wout_ref, post_g_ref, o_ref,
                 h_ref, glu_ref, cs_ref, cv_ref, pz_ref, so_ref, a_ref, acc_ref, yg_ref, yb_ref,
                 lin_ref, *, tile, d, ds):
    bf16, f32 = jnp.bfloat16, jnp.float32
    o_zc = 2 * d
    p_zs, p_gc, p_gs, p_end = d, d + ds, 2 * d + ds, 3 * d + ds
    nlane = d // LANES
    pieces = [slice(r, r + EW_ROWS) for r in range(0, tile, EW_ROWS)]

    @pl.when(pl.program_id(1) == 0)
    def _():
        cs_ref[:, 0:CONV_HALO, :] = jnp.zeros((nlane, CONV_HALO, LANES), f32)

    for rows in pieces:
        h_ref[rows, :] = (_rms_scale(x_ref[0, rows, :]) * pre_g_ref[...]).astype(bf16)

    glu_ref[...] = jnp.dot(h_ref[...], w_ref[:, 0:o_zc], preferred_element_type=f32)
    for rows in pieces:
        cu = glu_ref[rows, 0:d] * _sigmoid(glu_ref[rows, d:2 * d])
        for j in range(nlane):
            cs_ref[j, CONV_HALO + rows.start:CONV_HALO + rows.stop, :] = cu[:, j * LANES:(j + 1) * LANES]

    def ssm_glu():
        for rows in pieces:
            yg = _gelu_tanh(y_ref[0, rows, :])
            yg_ref[rows, :] = yg
            yb_ref[rows, :] = yg.astype(bf16)
        lin_ref[...] = jnp.dot(yb_ref[...], wglu_ref[...], preferred_element_type=f32)

    def ssm_out():
        for rows in pieces:
            gate = _sigmoid(lin_ref[rows, :] + bglu_ref[...]) * _silu(pz_ref[rows, p_zs:p_gc])
            yb_ref[rows, :] = (yg_ref[rows, :] * gate).astype(bf16)
        so_ref[...] = jnp.dot(yb_ref[...], wso_ref[...], preferred_element_type=f32)

    def conv_lane_block(j, lanes):
        for r in range(tile // CONV_ROWS):
            acc = jnp.broadcast_to(cb_ref[:, lanes], (CONV_ROWS, LANES))
            for lag in range(CONV_SIZE):
                lo = CONV_HALO + r * CONV_ROWS - lag
                tap = cw_ref[CONV_SIZE - 1 - lag:CONV_SIZE - lag, lanes]
                acc = acc + tap * cs_ref[j, lo:lo + CONV_ROWS, :]
            cv_ref[r * CONV_ROWS:(r + 1) * CONV_ROWS, lanes] = acc

    pz_ref[:, 0:p_zs] = jnp.dot(h_ref[...], w_ref[:, o_zc:o_zc + p_zs], preferred_element_type=f32)
    pz_ref[:, p_zs:p_end] = jnp.dot(h_ref[...], w_ref[:, o_zc + p_zs + ds:o_zc + p_end + ds],
                                    preferred_element_type=f32)
    ssm_glu()
    ssm_out()

    def conv_step(j, carry):
        conv_lane_block(j, pl.ds(pl.multiple_of(j * LANES, LANES), LANES))
        return carry

    lax.fori_loop(0, nlane, conv_step, 0)
    cs_ref[:, 0:CONV_HALO, :] = cs_ref[:, tile:CONV_HALO + tile, :]

    for rows in pieces:
        cv = cv_ref[rows, :]
        xc = cv - jnp.mean(cv, axis=-1, keepdims=True)
        var = jnp.mean(xc * xc, axis=-1, keepdims=True)
        ln = xc * lax.rsqrt(var + LN_EPS) * lng_ref[...] + lnb_ref[...]
        a_ref[rows, :] = (_silu(ln) * _silu(pz_ref[rows, 0:p_zs])).astype(bf16)
    acc_ref[...] = jnp.dot(a_ref[...], wco_ref[...], preferred_element_type=f32)

    for rows in pieces:
        merged = (_sigmoid(pz_ref[rows, p_gc:p_gs]) * acc_ref[rows, :]
                  + _sigmoid(pz_ref[rows, p_gs:p_end]) * so_ref[rows, :])
        a_ref[rows, :] = merged.astype(bf16)
    acc_ref[...] = jnp.dot(a_ref[...], wout_ref[...], preferred_element_type=f32)
    for rows in pieces:
        o_ref[0, rows, :] = x_ref[0, rows, :] + _rms_scale(acc_ref[rows, :]) * post_g_ref[...]


def _main(x, y, pre_g, w_in, cw, cb, lng, lnb, wco, wglu, bglu, wso, wout, post_g):
    b, l, d = x.shape
    ds = y.shape[-1]
    tile = MAIN_TILE
    assert CONV_HALO >= CONV_SIZE - 1 and CONV_HALO % SUBLANES == 0 and tile >= CONV_HALO
    kern = functools.partial(_main_kernel, tile=tile, d=d, ds=ds)
    consts = (pre_g, w_in, cw, cb, lng, lnb, wco, wglu, bglu, wso, wout, post_g)
    return pl.pallas_call(
        kern,
        grid=(b, l // tile),
        in_specs=[pl.BlockSpec((1, tile, d), lambda i, j: (i, j, 0)),
                  pl.BlockSpec((1, tile, ds), lambda i, j: (i, j, 0))]
                 + [_const_spec(a.shape) for a in consts],
        out_specs=pl.BlockSpec((1, tile, d), lambda i, j: (i, j, 0)),
        out_shape=jax.ShapeDtypeStruct((b, l, d), x.dtype),
        scratch_shapes=[pltpu.VMEM((tile, d), jnp.bfloat16),
                        pltpu.VMEM((tile, 2 * d), jnp.float32),
                        pltpu.VMEM((d // LANES, CONV_HALO + tile, LANES), jnp.float32),
                        pltpu.VMEM((tile, d), jnp.float32),
                        pltpu.VMEM((tile, 3 * d + ds), jnp.float32),
                        pltpu.VMEM((tile, d), jnp.float32),
                        pltpu.VMEM((tile, d), jnp.bfloat16),
                        pltpu.VMEM((tile, d), jnp.float32),
                        pltpu.VMEM((tile, ds), jnp.float32),
                        pltpu.VMEM((tile, ds), jnp.bfloat16),
                        pltpu.VMEM((tile, ds), jnp.float32)],
        compiler_params=pltpu.CompilerParams(dimension_semantics=("arbitrary", "arbitrary"),
                                             vmem_limit_bytes=VMEM_LIMIT),
        name="main_block",
    )(x, y, *consts)


def _layer(x, pre_g, w_in, conv_w, conv_b, ln_g, ln_b, w_conv_out, lam_re, lam_im, log_dt,
           b_re, b_im, c_re, c_im, d_skip, w_glu, b_glu, w_ssm_out, w_out, post_g):
    bsz, length, d = x.shape
    ds = w_glu.shape[0]
    bf16 = jnp.bfloat16
    o_us = 3 * d
    col = jnp.arange(w_in.shape[1])
    half_cols = ~((col < d) | ((col >= o_us) & (col < o_us + ds)))
    w_scale = jnp.where(half_cols, 0.5, 1.0).astype(w_in.dtype)
    row = lambda v: v.reshape(1, -1)
    assert lam_re.shape == (ds // SSM_GROUP, SSM_STATE)

    u4 = _us_proj(x.reshape(bsz * length, d), row(pre_g), w_in, o_us, ds)
    n_levels = SSM_ROWS.bit_length() - 1
    kb, fb, mb, ap, aq = _ssm_compact(lam_re, lam_im, log_dt, b_re, b_im, c_re, c_im, d_skip, n_levels)
    y, w_bf = _ssm_chunk(u4, kb, fb, mb, ap, aq, w_in, row(w_scale), length // CHUNK)
    y = y.reshape(bsz, length, ds)

    cw = jnp.concatenate([conv_w, jnp.zeros((SUBLANES - CONV_SIZE % SUBLANES, d), conv_w.dtype)], axis=0)
    return _main(x, y, row(pre_g), w_bf, cw, row(conv_b), row(0.5 * ln_g), row(0.5 * ln_b),
                 w_conv_out.astype(bf16), (0.5 * w_glu).astype(bf16), row(0.5 * b_glu),
                 w_ssm_out.astype(bf16), w_out.astype(bf16), row(post_g))


def kernel(x, pre_norm_gain, w_in, conv_w, conv_b, conv_ln_gain, conv_ln_bias, w_conv_out, ssm_lambda_re, ssm_lambda_im, ssm_log_dt, ssm_b_re, ssm_b_im, ssm_c_re, ssm_c_im, ssm_d, w_ssm_glu, b_ssm_glu, w_ssm_out, w_out, post_norm_gain):
    for l in range(pre_norm_gain.shape[0]):
        x = _layer(x, pre_norm_gain[l], w_in[l], conv_w[l], conv_b[l], conv_ln_gain[l],
                   conv_ln_bias[l], w_conv_out[l], ssm_lambda_re[l], ssm_lambda_im[l],
                   ssm_log_dt[l], ssm_b_re[l], ssm_b_im[l], ssm_c_re[l], ssm_c_im[l], ssm_d[l],
                   w_ssm_glu[l], b_ssm_glu[l], w_ssm_out[l], w_out[l], post_norm_gain[l])
    return x
```

```python
import functools
import math

import jax
import jax.numpy as jnp
from jax import lax
from jax.experimental import pallas as pl
from jax.experimental.pallas import tpu as pltpu

RMS_EPS = 1e-6
LN_EPS = 1e-5
CONV_SIZE = 31
SSM_GROUP = 16
SSM_STATE = 64
CHUNK = 8
LANES = 128
SUBLANES = 8
GROUPS_PER_BLOCK = LANES // SSM_GROUP
BLOCK_STATES = GROUPS_PER_BLOCK * SSM_STATE
ROW_WIDTH = CHUNK * LANES

US_TILE = 2048
SSM_ROWS = 512
MAIN_TILE = 512
CONV_HALO = 32
CONV_ROWS = 64
MXU_COLS = 256
EW_ROWS = 16
VMEM_LIMIT = 56 * 1024 * 1024


def _sigmoid(h):
    return 0.5 * jnp.tanh(h) + 0.5


def _silu(h):
    return h * jnp.tanh(h) + h


def _gelu_tanh(x):
    c = math.sqrt(2.0 / math.pi)
    return 0.5 * x * (1.0 + jnp.tanh(c * (x + 0.044715 * (x * x * x))))


def _rms_scale(x):
    return x * lax.rsqrt(jnp.mean(x * x, axis=-1, keepdims=True) + RMS_EPS)


def _const_spec(shape):
    zeros = (0,) * len(shape)
    return pl.BlockSpec(shape, lambda *_: zeros)


def _us_proj_kernel(x_ref, g_ref, w_ref, u_ref, slab_ref, *, tile, nblk):
    h = _rms_scale(x_ref[...]) * g_ref[...]
    u = jnp.dot(h.astype(jnp.bfloat16), w_ref[...].astype(jnp.bfloat16),
                preferred_element_type=jnp.float32)
    for b in range(nblk):
        slab_ref[b] = u[:, b * LANES:(b + 1) * LANES]
    for b in range(nblk):
        for k in range(CHUNK):
            u_ref[b, :, k * LANES:(k + 1) * LANES] = slab_ref[
                b, pl.ds(k, tile // CHUNK, stride=CHUNK), :].astype(u_ref.dtype)


def _us_proj(x2, pre_g, w_in, col, n):
    t, d = x2.shape
    nblk = n // LANES
    assert col % n == 0
    kern = functools.partial(_us_proj_kernel, tile=US_TILE, nblk=nblk)
    return pl.pallas_call(
        kern,
        grid=(t // US_TILE,),
        in_specs=[pl.BlockSpec((US_TILE, d), lambda i: (i, 0)),
                  _const_spec((1, d)),
                  pl.BlockSpec((d, n), lambda i: (0, col // n))],
        out_specs=pl.BlockSpec((nblk, US_TILE // CHUNK, ROW_WIDTH), lambda i: (0, i, 0)),
        out_shape=jax.ShapeDtypeStruct((nblk, t // CHUNK, ROW_WIDTH), jnp.bfloat16),
        scratch_shapes=[pltpu.VMEM((nblk, US_TILE, LANES), jnp.float32)],
        compiler_params=pltpu.CompilerParams(dimension_semantics=("arbitrary",),
                                             vmem_limit_bytes=VMEM_LIMIT),
        name="us_proj",
    )(x2, pre_g, w_in)


def _ssm_compact(lam_re, lam_im, log_dt, b_re, b_im, c_re, c_im, d, n_levels):
    groups, states = lam_re.shape
    gh = groups * SSM_GROUP
    dt = jnp.exp(log_dt)[:, None]
    lr, li = lam_re, lam_im
    scan_pows = [CHUNK << lev for lev in range(n_levels)]
    row_pows = [CHUNK * r for r in range(SUBLANES)]
    nv = jnp.asarray(list(range(CHUNK + 1)) + scan_pows + row_pows, jnp.float32)[:, None, None]
    mag = jnp.exp(nv * (lr * dt))
    pw_r = mag * jnp.cos(nv * (li * dt))
    pw_i = mag * jnp.sin(nv * (li * dt))
    pr, pi = pw_r[:CHUNK + 1], pw_i[:CHUNK + 1]
    ar, ai = pr[1], pi[1]
    den = lr * lr + li * li
    zr = ((ar - 1.0) * lr + ai * li) / den
    zi = (ai * lr - (ar - 1.0) * li) / den
    bbr = zr[..., None] * b_re - zi[..., None] * b_im
    bbi = zr[..., None] * b_im + zi[..., None] * b_re

    bt_r = bbr.transpose(2, 0, 1).reshape(1, SSM_GROUP, groups * states)
    bt_i = bbi.transpose(2, 0, 1).reshape(1, SSM_GROUP, groups * states)
    qr = pr[CHUNK - 1::-1].reshape(CHUNK, 1, groups * states)
    qi = pi[CHUNK - 1::-1].reshape(CHUNK, 1, groups * states)
    fb = (qr * bt_r - qi * bt_i, qr * bt_i + qi * bt_r)

    spread = lambda v: jnp.repeat(jnp.swapaxes(v, -1, -2), SSM_GROUP, axis=-1)
    px_r, px_i = spread(pr), spread(pi)
    ct_r = c_re.transpose(2, 0, 1).reshape(1, states, gh)
    ct_i = c_im.transpose(2, 0, 1).reshape(1, states, gh)
    m_r = ct_r * px_r - ct_i * px_i
    m_i = ct_r * px_i + ct_i * px_r
    mb = (m_r[1:], -m_i[1:])
    bx_r = spread(bbr.transpose(2, 0, 1))
    bx_i = spread(bbi.transpose(2, 0, 1))
    kb = jnp.sum(m_r[:CHUNK, None] * bx_r[None] - m_i[:CHUNK, None] * bx_i[None], axis=2)
    skip = (d.T[:, :, None] * jnp.eye(SSM_GROUP, dtype=d.dtype)[:, None, :]).reshape(SSM_GROUP, gh)
    kb = kb.at[0].add(skip)

    nblk = groups // GROUPS_PER_BLOCK
    lo, hi = CHUNK + 1, CHUNK + 1 + n_levels
    ap = jnp.stack([pw_r[lo:hi], pw_i[lo:hi]], axis=1)
    ap = ap.reshape(2 * n_levels, nblk, BLOCK_STATES).transpose(1, 0, 2)
    aq = jnp.concatenate([pw_r[hi:], pw_i[hi:]])
    aq = aq.reshape(2 * SUBLANES, nblk, BLOCK_STATES).transpose(1, 0, 2)
    return kb, fb, mb, ap, aq


def _block_diag(compact, group_rows, group_cols):
    n = compact.shape[1]
    tiled = jnp.concatenate([compact] * GROUPS_PER_BLOCK, axis=0)
    r = lax.broadcasted_iota(jnp.int32, tiled.shape, 0) // group_rows
    c = lax.broadcasted_iota(jnp.int32, tiled.shape, 1) // group_cols
    return jnp.where(r == c, tiled, 0.0).astype(jnp.bfloat16)


def _ssm_chunk_kernel(u_ref, kb_ref, fbr_ref, fbi_ref, mbr_ref, mbi_ref, ap_ref, aq_ref, wf_ref, ws_ref,
                      y_ref, wbf_ref,
                      tz_ref, wb_ref, wc_ref, cre_ref, cim_ref, lre_ref, lim_ref, ebr_ref, ebi_ref,
                      *, rows, n_levels, blocks_per_seq):
    half = BLOCK_STATES
    wbf_ref[...] = (wf_ref[...] * ws_ref[...]).astype(wbf_ref.dtype)

    @pl.when(pl.program_id(1) == 0)
    def _():
        tz_ref[...] = jnp.zeros_like(tz_ref)
        for lag in range(CHUNK):
            blk = _block_diag(kb_ref[lag], SSM_GROUP, SSM_GROUP)
            for k in range(CHUNK - lag):
                t = k + lag
                tz_ref[k * LANES:(k + 1) * LANES, t * LANES:(t + 1) * LANES] = blk
        for ri, (fb_ref, mb_ref) in enumerate(((fbr_ref, mbr_ref), (fbi_ref, mbi_ref))):
            for k in range(CHUNK):
                wb_ref[k * LANES:(k + 1) * LANES, ri * half:(ri + 1) * half] = _block_diag(
                    fb_ref[k], SSM_GROUP, SSM_STATE)
                wc_ref[ri * half:(ri + 1) * half, k * LANES:(k + 1) * LANES] = _block_diag(
                    mb_ref[k], SSM_STATE, SSM_GROUP)

    @pl.when(pl.program_id(1) % blocks_per_seq == 0)
    def _():
        cre_ref[...] = jnp.zeros_like(cre_ref)
        cim_ref[...] = jnp.zeros_like(cim_ref)

    def scan_level(vr, vi, lev, shift, pos):
        mr, mi = ap_ref[0, 2 * lev:2 * lev + 1, :], ap_ref[0, 2 * lev + 1:2 * lev + 2, :]
        keep = pos >= shift
        sr = jnp.where(keep, pltpu.roll(vr, shift, 0), 0.0)
        si = jnp.where(keep, pltpu.roll(vi, shift, 0), 0.0)
        return vr + (mr * sr - mi * si), vi + (mr * si + mi * sr)

    u = u_ref[0]
    y1 = jnp.dot(u, tz_ref[...], preferred_element_type=jnp.float32)
    bc = jnp.dot(u, wb_ref[...], preferred_element_type=jnp.float32)
    n_grp = rows // SUBLANES
    in_levels = SUBLANES.bit_length() - 1
    assert n_grp == 1 << (n_levels - in_levels)

    tiles = lambda v: v.reshape(n_grp, SUBLANES, half)
    lre, lim = tiles(bc[:, :half]), tiles(bc[:, half:])
    rin = lax.broadcasted_iota(jnp.int32, (1, SUBLANES, 1), 1)
    for lev in range(in_levels):
        s = 1 << lev
        mr = jnp.where(rin >= s, ap_ref[0, 2 * lev:2 * lev + 1, :][None], 0.0)
        mi = jnp.where(rin >= s, ap_ref[0, 2 * lev + 1:2 * lev + 2, :][None], 0.0)
        sr, si = pltpu.roll(lre, s, 1), pltpu.roll(lim, s, 1)
        lre, lim = lre + (mr * sr - mi * si), lim + (mr * si + mi * sr)
    lsh_re = jnp.where(rin >= 1, pltpu.roll(lre, 1, 1), 0.0).reshape(rows, half)
    lsh_im = jnp.where(rin >= 1, pltpu.roll(lim, 1, 1), 0.0).reshape(rows, half)
    lre, lim = lre.reshape(rows, half), lim.reshape(rows, half)
    nslab = half // LANES
    slabs = [slice(q * LANES, (q + 1) * LANES) for q in range(nslab)]
    for q in range(nslab):
        lre_ref[q] = lre[:, slabs[q]]
        lim_ref[q] = lim[:, slabs[q]]

    ends = pl.ds(SUBLANES - 1, n_grp, stride=SUBLANES)
    ere = jnp.concatenate([lre_ref[q, ends, :] for q in range(nslab)], axis=1)
    eim = jnp.concatenate([lim_ref[q, ends, :] for q in range(nslab)], axis=1)
    grp = lax.broadcasted_iota(jnp.int32, (n_grp, 1), 0)
    cre, cim = cre_ref[...], cim_ref[...]
    gr, gi = ap_ref[0, 2 * in_levels:2 * in_levels + 1, :], ap_ref[0, 2 * in_levels + 1:2 * in_levels + 2, :]
    ere = ere + jnp.where(grp == 0, gr * cre - gi * cim, 0.0)
    eim = eim + jnp.where(grp == 0, gr * cim + gi * cre, 0.0)
    for lev in range(in_levels, n_levels):
        ere, eim = scan_level(ere, eim, lev, 1 << (lev - in_levels), grp)
    pre = jnp.where(grp == 0, cre, pltpu.roll(ere, 1, 0))
    pim = jnp.where(grp == 0, cim, pltpu.roll(eim, 1, 0))
    cre_ref[...] = ere[n_grp - 1:n_grp, :]
    cim_ref[...] = eim[n_grp - 1:n_grp, :]

    for q in range(nslab):
        for r in range(SUBLANES):
            ebr_ref[q, pl.ds(r, n_grp, stride=SUBLANES), :] = pre[:, slabs[q]]
            ebi_ref[q, pl.ds(r, n_grp, stride=SUBLANES), :] = pim[:, slabs[q]]
    ebr = jnp.concatenate([ebr_ref[q] for q in range(nslab)], axis=1)
    ebi = jnp.concatenate([ebi_ref[q] for q in range(nslab)], axis=1)
    qr = jnp.concatenate([aq_ref[0, 0:SUBLANES, :]] * n_grp, axis=0)
    qi = jnp.concatenate([aq_ref[0, SUBLANES:2 * SUBLANES, :]] * n_grp, axis=0)
    st_re = lsh_re + (qr * ebr - qi * ebi)
    st_im = lsh_im + (qr * ebi + qi * ebr)
    sst = jnp.concatenate([st_re, st_im], axis=1).astype(jnp.bfloat16)
    y = y1 + jnp.dot(sst, wc_ref[...], preferred_element_type=jnp.float32)
    for t in range(CHUNK):
        y_ref[pl.ds(t, rows, stride=CHUNK), :] = y[:, t * LANES:(t + 1) * LANES]


def _ssm_chunk(u4, kb, fb, mb, ap, aq, w_f32, w_scale, rows_per_seq):
    nblk, r, size = u4.shape
    n_levels = ap.shape[1] // 2
    assert SSM_ROWS == 1 << n_levels and rows_per_seq % SSM_ROWS == 0 and size == ROW_WIDTH
    nb = r // SSM_ROWS
    wrows, wcols = w_f32.shape
    wblk = wcols // (nblk * nb)
    assert wblk * nblk * nb == wcols and wblk % LANES == 0
    kern = functools.partial(_ssm_chunk_kernel, rows=SSM_ROWS, n_levels=n_levels,
                             blocks_per_seq=rows_per_seq // SSM_ROWS)
    mat = pltpu.VMEM((ROW_WIDTH, ROW_WIDTH), jnp.bfloat16)
    state = pltpu.VMEM((BLOCK_STATES // LANES, SSM_ROWS, LANES), jnp.float32)
    assert 2 * BLOCK_STATES == ROW_WIDTH
    return pl.pallas_call(
        kern,
        grid=(nblk, r // SSM_ROWS),
        in_specs=[pl.BlockSpec((1, SSM_ROWS, size), lambda g, i: (g, i, 0)),
                  pl.BlockSpec((CHUNK, SSM_GROUP, LANES), lambda g, i: (0, 0, g)),
                  pl.BlockSpec((CHUNK, SSM_GROUP, BLOCK_STATES), lambda g, i: (0, 0, g)),
                  pl.BlockSpec((CHUNK, SSM_GROUP, BLOCK_STATES), lambda g, i: (0, 0, g)),
                  pl.BlockSpec((CHUNK, SSM_STATE, LANES), lambda g, i: (0, 0, g)),
                  pl.BlockSpec((CHUNK, SSM_STATE, LANES), lambda g, i: (0, 0, g)),
                  pl.BlockSpec((1,) + ap.shape[1:], lambda g, i: (g, 0, 0)),
                  pl.BlockSpec((1,) + aq.shape[1:], lambda g, i: (g, 0, 0)),
                  pl.BlockSpec((wrows, wblk), lambda g, i: (0, g * nb + i)),
                  pl.BlockSpec((1, wblk), lambda g, i: (0, g * nb + i))],
        out_specs=[pl.BlockSpec((SSM_ROWS * CHUNK, LANES), lambda g, i: (i, g)),
                   pl.BlockSpec((wrows, wblk), lambda g, i: (0, g * nb + i))],
        out_shape=[jax.ShapeDtypeStruct((r * CHUNK, nblk * LANES), jnp.float32),
                   jax.ShapeDtypeStruct((wrows, wcols), jnp.bfloat16)],
        scratch_shapes=[mat, mat, mat,
                        pltpu.VMEM((1, BLOCK_STATES), jnp.float32),
                        pltpu.VMEM((1, BLOCK_STATES), jnp.float32),
                        state, state, state, state],
        compiler_params=pltpu.CompilerParams(dimension_semantics=("arbitrary", "arbitrary"),
                                             vmem_limit_bytes=VMEM_LIMIT),
        name="ssm_chunk",
    )(u4, kb, *fb, *mb, ap, aq, w_f32, w_scale)


def _main_kernel(x_ref, y_ref, pre_g_ref, w_ref, cw_ref, cb_ref, lng_ref, lnb_ref, wco_ref,
                 wglu_ref, bglu_ref, wso_ref, wout_ref, post_g_ref, o_ref,
                 h_ref, glu_ref, cs_ref, cv_ref, pz_ref, so_ref, a_ref, acc_ref, yg_ref, yb_ref,
                 lin_ref, *, tile, d, ds):
    bf16, f32 = jnp.bfloat16, jnp.float32
    o_zc = 2 * d
    p_zs, p_gc, p_gs, p_end = d, d + ds, 2 * d + ds, 3 * d + ds
    nlane = d // LANES
    pieces = [slice(r, r + EW_ROWS) for r in range(0, tile, EW_ROWS)]

    @pl.when(pl.program_id(1) == 0)
    def _():
        cs_ref[:, 0:CONV_HALO, :] = jnp.zeros((nlane, CONV_HALO, LANES), f32)

    for rows in pieces:
        h_ref[rows, :] = (_rms_scale(x_ref[0, rows, :]) * pre_g_ref[...]).astype(bf16)

    glu_ref[...] = jnp.dot(h_ref[...], w_ref[:, 0:o_zc], preferred_element_type=f32)
    for rows in pieces:
        cu = glu_ref[rows, 0:d] * _sigmoid(glu_ref[rows, d:2 * d])
        for j in range(nlane):
            cs_ref[j, CONV_HALO + rows.start:CONV_HALO + rows.stop, :] = cu[:, j * LANES:(j + 1) * LANES]

    def ssm_glu():
        for rows in pieces:
            yg = _gelu_tanh(y_ref[0, rows, :])
            yg_ref[rows, :] = yg
            yb_ref[rows, :] = yg.astype(bf16)
        lin_ref[...] = jnp.dot(yb_ref[...], wglu_ref[...], preferred_element_type=f32)

    def ssm_out():
        for rows in pieces:
            gate = _sigmoid(lin_ref[rows, :] + bglu_ref[...]) * _silu(pz_ref[rows, p_zs:p_gc])
            yb_ref[rows, :] = (yg_ref[rows, :] * gate).astype(bf16)
        so_ref[...] = jnp.dot(yb_ref[...], wso_ref[...], preferred_element_type=f32)

    def conv_lane_block(j, lanes):
        for r in range(tile // CONV_ROWS):
            acc = jnp.broadcast_to(cb_ref[:, lanes], (CONV_ROWS, LANES))
            for lag in range(CONV_SIZE):
                lo = CONV_HALO + r * CONV_ROWS - lag
                tap = cw_ref[CONV_SIZE - 1 - lag:CONV_SIZE - lag, lanes]
                acc = acc + tap * cs_ref[j, lo:lo + CONV_ROWS, :]
            cv_ref[r * CONV_ROWS:(r + 1) * CONV_ROWS, lanes] = acc

    pz_ref[:, 0:p_zs] = jnp.dot(h_ref[...], w_ref[:, o_zc:o_zc + p_zs], preferred_element_type=f32)
    pz_ref[:, p_zs:p_end] = jnp.dot(h_ref[...], w_ref[:, o_zc + p_zs + ds:o_zc + p_end + ds],
                                    preferred_element_type=f32)
    ssm_glu()
    ssm_out()

    def conv_step(j, carry):
        conv_lane_block(j, pl.ds(pl.multiple_of(j * LANES, LANES), LANES))
        return carry

    lax.fori_loop(0, nlane, conv_step, 0)
    cs_ref[:, 0:CONV_HALO, :] = cs_ref[:, tile:CONV_HALO + tile, :]

    for rows in pieces:
        cv = cv_ref[rows, :]
        xc = cv - jnp.mean(cv, axis=-1, keepdims=True)
        var = jnp.mean(xc * xc, axis=-1, keepdims=True)
        ln = xc * lax.rsqrt(var + LN_EPS) * lng_ref[...] + lnb_ref[...]
        a_ref[rows, :] = (_silu(ln) * _silu(pz_ref[rows, 0:p_zs])).astype(bf16)
    acc_ref[...] = jnp.dot(a_ref[...], wco_ref[...], preferred_element_type=f32)

    for rows in pieces:
        merged = (_sigmoid(pz_ref[rows, p_gc:p_gs]) * acc_ref[rows, :]
                  + _sigmoid(pz_ref[rows, p_gs:p_end]) * so_ref[rows, :])
        a_ref[rows, :] = merged.astype(bf16)
    acc_ref[...] = jnp.dot(a_ref[...], wout_ref[...], preferred_element_type=f32)
    for rows in pieces:
        o_ref[0, rows, :] = x_ref[0, rows, :] + _rms_scale(acc_ref[rows, :]) * post_g_ref[...]


def _main(x, y, pre_g, w_in, cw, cb, lng, lnb, wco, wglu, bglu, wso, wout, post_g):
    b, l, d = x.shape
    ds = y.shape[-1]
    tile = MAIN_TILE
    assert CONV_HALO >= CONV_SIZE - 1 and CONV_HALO % SUBLANES == 0 and tile >= CONV_HALO
    kern = functools.partial(_main_kernel, tile=tile, d=d, ds=ds)
    consts = (pre_g, w_in, cw, cb, lng, lnb, wco, wglu, bglu, wso, wout, post_g)
    return pl.pallas_call(
        kern,
        grid=(b, l // tile),
        in_specs=[pl.BlockSpec((1, tile, d), lambda i, j: (i, j, 0)),
                  pl.BlockSpec((1, tile, ds), lambda i, j: (i, j, 0))]
                 + [_const_spec(a.shape) for a in consts],
        out_specs=pl.BlockSpec((1, tile, d), lambda i, j: (i, j, 0)),
        out_shape=jax.ShapeDtypeStruct((b, l, d), x.dtype),
        scratch_shapes=[pltpu.VMEM((tile, d), jnp.bfloat16),
                        pltpu.VMEM((tile, 2 * d), jnp.float32),
                        pltpu.VMEM((d // LANES, CONV_HALO + tile, LANES), jnp.float32),
                        pltpu.VMEM((tile, d), jnp.float32),
                        pltpu.VMEM((tile, 3 * d + ds), jnp.float32),
                        pltpu.VMEM((tile, d), jnp.float32),
                        pltpu.VMEM((tile, d), jnp.bfloat16),
                        pltpu.VMEM((tile, d), jnp.float32),
                        pltpu.VMEM((tile, ds), jnp.float32),
                        pltpu.VMEM((tile, ds), jnp.bfloat16),
                        pltpu.VMEM((tile, ds), jnp.float32)],
        compiler_params=pltpu.CompilerParams(dimension_semantics=("arbitrary", "arbitrary"),
                                             vmem_limit_bytes=VMEM_LIMIT),
        name="main_block",
    )(x, y, *consts)


def _layer(x, pre_g, w_in, conv_w, conv_b, ln_g, ln_b, w_conv_out, lam_re, lam_im, log_dt,
           b_re, b_im, c_re, c_im, d_skip, w_glu, b_glu, w_ssm_out, w_out, post_g):
    bsz, length, d = x.shape
    ds = w_glu.shape[0]
    bf16 = jnp.bfloat16
    o_us = 3 * d
    col = jnp.arange(w_in.shape[1])
    half_cols = ~((col < d) | ((col >= o_us) & (col < o_us + ds)))
    w_scale = jnp.where(half_cols, 0.5, 1.0).astype(w_in.dtype)
    row = lambda v: v.reshape(1, -1)
    assert lam_re.shape == (ds // SSM_GROUP, SSM_STATE)

    u4 = _us_proj(x.reshape(bsz * length, d), row(pre_g), w_in, o_us, ds)
    n_levels = SSM_ROWS.bit_length() - 1
    kb, fb, mb, ap, aq = _ssm_compact(lam_re, lam_im, log_dt, b_re, b_im, c_re, c_im, d_skip, n_levels)
    y, w_bf = _ssm_chunk(u4, kb, fb, mb, ap, aq, w_in, row(w_scale), length // CHUNK)
    y = y.reshape(bsz, length, ds)

    cw = jnp.concatenate([conv_w, jnp.zeros((SUBLANES - CONV_SIZE % SUBLANES, d), conv_w.dtype)], axis=0)
    return _main(x, y, row(pre_g), w_bf, cw, row(conv_b), row(0.5 * ln_g), row(0.5 * ln_b),
                 w_conv_out.astype(bf16), (0.5 * w_glu).astype(bf16), row(0.5 * b_glu),
                 w_ssm_out.astype(bf16), w_out.astype(bf16), row(post_g))


def kernel(x, pre_norm_gain, w_in, conv_w, conv_b, conv_ln_gain, conv_ln_bias, w_conv_out, ssm_lambda_re, ssm_lambda_im, ssm_log_dt, ssm_b_re, ssm_b_im, ssm_c_re, ssm_c_im, ssm_d, w_ssm_glu, b_ssm_glu, w_ssm_out, w_out, post_norm_gain):
    for l in range(pre_norm_gain.shape[0]):
        x = _layer(x, pre_norm_gain[l], w_in[l], conv_w[l], conv_b[l], conv_ln_gain[l],
                   conv_ln_bias[l], w_conv_out[l], ssm_lambda_re[l], ssm_lambda_im[l],
                   ssm_log_dt[l], ssm_b_re[l], ssm_b_im[l], ssm_c_re[l], ssm_c_im[l], ssm_d[l],
                   w_ssm_glu[l], b_ssm_glu[l], w_ssm_out[l], w_out[l], post_norm_gain[l])
    return x
```

```python
import functools
import math

import jax
import jax.numpy as jnp
from jax import lax
from jax.experimental import pallas as pl
from jax.experimental.pallas import tpu as pltpu

RMS_EPS = 1e-6
LN_EPS = 1e-5
CONV_SIZE = 31
SSM_GROUP = 16
SSM_STATE = 64
CHUNK = 8
LANES = 128
SUBLANES = 8
GROUPS_PER_BLOCK = LANES // SSM_GROUP
BLOCK_STATES = GROUPS_PER_BLOCK * SSM_STATE
ROW_WIDTH = CHUNK * LANES

US_TILE = 2048
SSM_ROWS = 512
MAIN_TILE = 512
CONV_HALO = 32
CONV_ROWS = 64
MXU_COLS = 256
EW_ROWS = 16
VMEM_LIMIT = 56 * 1024 * 1024


def _sigmoid(h):
    return 0.5 * jnp.tanh(h) + 0.5


def _silu(h):
    return h * jnp.tanh(h) + h


def _gelu_tanh(x):
    c = math.sqrt(2.0 / math.pi)
    return 0.5 * x * (1.0 + jnp.tanh(c * (x + 0.044715 * (x * x * x))))


def _rms_scale(x):
    return x * lax.rsqrt(jnp.mean(x * x, axis=-1, keepdims=True) + RMS_EPS)


def _const_spec(shape):
    zeros = (0,) * len(shape)
    return pl.BlockSpec(shape, lambda *_: zeros)


def _us_proj_kernel(x_ref, g_ref, w_ref, u_ref, h_ref, slab_ref, *, tile, nblk):
    h_ref[...] = (_rms_scale(x_ref[...]) * g_ref[...]).astype(h_ref.dtype)
    u = jnp.dot(h_ref[...], w_ref[...].astype(jnp.bfloat16), preferred_element_type=jnp.float32)
    for b in range(nblk):
        slab_ref[b] = u[:, b * LANES:(b + 1) * LANES]
    for b in range(nblk):
        for k in range(CHUNK):
            u_ref[b, :, k * LANES:(k + 1) * LANES] = slab_ref[
                b, pl.ds(k, tile // CHUNK, stride=CHUNK), :].astype(u_ref.dtype)


def _us_proj(x2, pre_g, w_in, col, n):
    t, d = x2.shape
    nblk = n // LANES
    assert col % n == 0
    kern = functools.partial(_us_proj_kernel, tile=US_TILE, nblk=nblk)
    return pl.pallas_call(
        kern,
        grid=(t // US_TILE,),
        in_specs=[pl.BlockSpec((US_TILE, d), lambda i: (i, 0)),
                  _const_spec((1, d)),
                  pl.BlockSpec((d, n), lambda i: (0, col // n))],
        out_specs=[pl.BlockSpec((nblk, US_TILE // CHUNK, ROW_WIDTH), lambda i: (0, i, 0)),
                   pl.BlockSpec((US_TILE, d), lambda i: (i, 0))],
        out_shape=[jax.ShapeDtypeStruct((nblk, t // CHUNK, ROW_WIDTH), jnp.bfloat16),
                   jax.ShapeDtypeStruct((t, d), jnp.bfloat16)],
        scratch_shapes=[pltpu.VMEM((nblk, US_TILE, LANES), jnp.float32)],
        compiler_params=pltpu.CompilerParams(dimension_semantics=("arbitrary",),
                                             vmem_limit_bytes=VMEM_LIMIT),
        name="us_proj",
    )(x2, pre_g, w_in)


def _ssm_compact(lam_re, lam_im, log_dt, b_re, b_im, c_re, c_im, d, n_levels):
    groups, states = lam_re.shape
    gh = groups * SSM_GROUP
    dt = jnp.exp(log_dt)[:, None]
    lr, li = lam_re, lam_im
    scan_pows = [CHUNK << lev for lev in range(n_levels)]
    row_pows = [CHUNK * r for r in range(SUBLANES)]
    nv = jnp.asarray(list(range(CHUNK + 1)) + scan_pows + row_pows, jnp.float32)[:, None, None]
    mag = jnp.exp(nv * (lr * dt))
    pw_r = mag * jnp.cos(nv * (li * dt))
    pw_i = mag * jnp.sin(nv * (li * dt))
    pr, pi = pw_r[:CHUNK + 1], pw_i[:CHUNK + 1]
    ar, ai = pr[1], pi[1]
    den = lr * lr + li * li
    zr = ((ar - 1.0) * lr + ai * li) / den
    zi = (ai * lr - (ar - 1.0) * li) / den
    bbr = zr[..., None] * b_re - zi[..., None] * b_im
    bbi = zr[..., None] * b_im + zi[..., None] * b_re

    bt_r = bbr.transpose(2, 0, 1).reshape(1, SSM_GROUP, groups * states)
    bt_i = bbi.transpose(2, 0, 1).reshape(1, SSM_GROUP, groups * states)
    qr = pr[CHUNK - 1::-1].reshape(CHUNK, 1, groups * states)
    qi = pi[CHUNK - 1::-1].reshape(CHUNK, 1, groups * states)
    fb = (qr * bt_r - qi * bt_i, qr * bt_i + qi * bt_r)

    spread = lambda v: jnp.repeat(jnp.swapaxes(v, -1, -2), SSM_GROUP, axis=-1)
    px_r, px_i = spread(pr), spread(pi)
    ct_r = c_re.transpose(2, 0, 1).reshape(1, states, gh)
    ct_i = c_im.transpose(2, 0, 1).reshape(1, states, gh)
    m_r = ct_r * px_r - ct_i * px_i
    m_i = ct_r * px_i + ct_i * px_r
    mb = (m_r[1:], -m_i[1:])
    bx_r = spread(bbr.transpose(2, 0, 1))
    bx_i = spread(bbi.transpose(2, 0, 1))
    kb = jnp.sum(m_r[:CHUNK, None] * bx_r[None] - m_i[:CHUNK, None] * bx_i[None], axis=2)
    skip = (d.T[:, :, None] * jnp.eye(SSM_GROUP, dtype=d.dtype)[:, None, :]).reshape(SSM_GROUP, gh)
    kb = kb.at[0].add(skip)

    nblk = groups // GROUPS_PER_BLOCK
    lo, hi = CHUNK + 1, CHUNK + 1 + n_levels
    ap = jnp.stack([pw_r[lo:hi], pw_i[lo:hi]], axis=1)
    ap = ap.reshape(2 * n_levels, nblk, BLOCK_STATES).transpose(1, 0, 2)
    aq = jnp.concatenate([pw_r[hi:], pw_i[hi:]])
    aq = aq.reshape(2 * SUBLANES, nblk, BLOCK_STATES).transpose(1, 0, 2)
    return kb, fb, mb, ap, aq


def _block_diag(compact, group_rows, group_cols):
    n = compact.shape[1]
    tiled = jnp.concatenate([compact] * GROUPS_PER_BLOCK, axis=0)
    r = lax.broadcasted_iota(jnp.int32, tiled.shape, 0) // group_rows
    c = lax.broadcasted_iota(jnp.int32, tiled.shape, 1) // group_cols
    return jnp.where(r == c, tiled, 0.0).astype(jnp.bfloat16)


def _ssm_chunk_kernel(u_ref, kb_ref, fbr_ref, fbi_ref, mbr_ref, mbi_ref, ap_ref, aq_ref, wf_ref, ws_ref,
                      y_ref, wbf_ref,
                      tz_ref, wb_ref, wc_ref, cre_ref, cim_ref, lre_ref, lim_ref, ebr_ref, ebi_ref,
                      *, rows, n_levels, blocks_per_seq):
    half = BLOCK_STATES
    wbf_ref[...] = (wf_ref[...] * ws_ref[...]).astype(wbf_ref.dtype)

    @pl.when(pl.program_id(1) == 0)
    def _():
        tz_ref[...] = jnp.zeros_like(tz_ref)
        for lag in range(CHUNK):
            blk = _block_diag(kb_ref[lag], SSM_GROUP, SSM_GROUP)
            for k in range(CHUNK - lag):
                t = k + lag
                tz_ref[k * LANES:(k + 1) * LANES, t * LANES:(t + 1) * LANES] = blk
        for ri, (fb_ref, mb_ref) in enumerate(((fbr_ref, mbr_ref), (fbi_ref, mbi_ref))):
            for k in range(CHUNK):
                wb_ref[k * LANES:(k + 1) * LANES, ri * half:(ri + 1) * half] = _block_diag(
                    fb_ref[k], SSM_GROUP, SSM_STATE)
                wc_ref[ri * half:(ri + 1) * half, k * LANES:(k + 1) * LANES] = _block_diag(
                    mb_ref[k], SSM_STATE, SSM_GROUP)

    @pl.when(pl.program_id(1) % blocks_per_seq == 0)
    def _():
        cre_ref[...] = jnp.zeros_like(cre_ref)
        cim_ref[...] = jnp.zeros_like(cim_ref)

    def scan_level(vr, vi, lev, shift, pos):
        mr, mi = ap_ref[0, 2 * lev:2 * lev + 1, :], ap_ref[0, 2 * lev + 1:2 * lev + 2, :]
        keep = pos >= shift
        sr = jnp.where(keep, pltpu.roll(vr, shift, 0), 0.0)
        si = jnp.where(keep, pltpu.roll(vi, shift, 0), 0.0)
        return vr + (mr * sr - mi * si), vi + (mr * si + mi * sr)

    u = u_ref[0]
    y1 = jnp.dot(u, tz_ref[...], preferred_element_type=jnp.float32)
    bc = jnp.dot(u, wb_ref[...], preferred_element_type=jnp.float32)
    n_grp = rows // SUBLANES
    in_levels = SUBLANES.bit_length() - 1
    assert n_grp == 1 << (n_levels - in_levels)

    tiles = lambda v: v.reshape(n_grp, SUBLANES, half)
    lre, lim = tiles(bc[:, :half]), tiles(bc[:, half:])
    rin = lax.broadcasted_iota(jnp.int32, (1, SUBLANES, 1), 1)
    for lev in range(in_levels):
        s = 1 << lev
        mr = jnp.where(rin >= s, ap_ref[0, 2 * lev:2 * lev + 1, :][None], 0.0)
        mi = jnp.where(rin >= s, ap_ref[0, 2 * lev + 1:2 * lev + 2, :][None], 0.0)
        sr, si = pltpu.roll(lre, s, 1), pltpu.roll(lim, s, 1)
        lre, lim = lre + (mr * sr - mi * si), lim + (mr * si + mi * sr)
    lsh_re = jnp.where(rin >= 1, pltpu.roll(lre, 1, 1), 0.0).reshape(rows, half)
    lsh_im = jnp.where(rin >= 1, pltpu.roll(lim, 1, 1), 0.0).reshape(rows, half)
    lre, lim = lre.reshape(rows, half), lim.reshape(rows, half)
    nslab = half // LANES
    slabs = [slice(q * LANES, (q + 1) * LANES) for q in range(nslab)]
    for q in range(nslab):
        lre_ref[q] = lre[:, slabs[q]]
        lim_ref[q] = lim[:, slabs[q]]

    ends = pl.ds(SUBLANES - 1, n_grp, stride=SUBLANES)
    ere = jnp.concatenate([lre_ref[q, ends, :] for q in range(nslab)], axis=1)
    eim = jnp.concatenate([lim_ref[q, ends, :] for q in range(nslab)], axis=1)
    grp = lax.broadcasted_iota(jnp.int32, (n_grp, 1), 0)
    cre, cim = cre_ref[...], cim_ref[...]
    gr, gi = ap_ref[0, 2 * in_levels:2 * in_levels + 1, :], ap_ref[0, 2 * in_levels + 1:2 * in_levels + 2, :]
    ere = ere + jnp.where(grp == 0, gr * cre - gi * cim, 0.0)
    eim = eim + jnp.where(grp == 0, gr * cim + gi * cre, 0.0)
    for lev in range(in_levels, n_levels):
        ere, eim = scan_level(ere, eim, lev, 1 << (lev - in_levels), grp)
    pre = jnp.where(grp == 0, cre, pltpu.roll(ere, 1, 0))
    pim = jnp.where(grp == 0, cim, pltpu.roll(eim, 1, 0))
    cre_ref[...] = ere[n_grp - 1:n_grp, :]
    cim_ref[...] = eim[n_grp - 1:n_grp, :]

    for q in range(nslab):
        for r in range(SUBLANES):
            ebr_ref[q, pl.ds(r, n_grp, stride=SUBLANES), :] = pre[:, slabs[q]]
            ebi_ref[q, pl.ds(r, n_grp, stride=SUBLANES), :] = pim[:, slabs[q]]
    ebr = jnp.concatenate([ebr_ref[q] for q in range(nslab)], axis=1)
    ebi = jnp.concatenate([ebi_ref[q] for q in range(nslab)], axis=1)
    qr = jnp.concatenate([aq_ref[0, 0:SUBLANES, :]] * n_grp, axis=0)
    qi = jnp.concatenate([aq_ref[0, SUBLANES:2 * SUBLANES, :]] * n_grp, axis=0)
    st_re = lsh_re + (qr * ebr - qi * ebi)
    st_im = lsh_im + (qr * ebi + qi * ebr)
    sst = jnp.concatenate([st_re, st_im], axis=1).astype(jnp.bfloat16)
    y = y1 + jnp.dot(sst, wc_ref[...], preferred_element_type=jnp.float32)
    for t in range(CHUNK):
        y_ref[pl.ds(t, rows, stride=CHUNK), :] = y[:, t * LANES:(t + 1) * LANES]


def _ssm_chunk(u4, kb, fb, mb, ap, aq, w_f32, w_scale, rows_per_seq):
    nblk, r, size = u4.shape
    n_levels = ap.shape[1] // 2
    assert SSM_ROWS == 1 << n_levels and rows_per_seq % SSM_ROWS == 0 and size == ROW_WIDTH
    nb = r // SSM_ROWS
    wrows, wcols = w_f32.shape
    wblk = wcols // (nblk * nb)
    assert wblk * nblk * nb == wcols and wblk % LANES == 0
    kern = functools.partial(_ssm_chunk_kernel, rows=SSM_ROWS, n_levels=n_levels,
                             blocks_per_seq=rows_per_seq // SSM_ROWS)
    mat = pltpu.VMEM((ROW_WIDTH, ROW_WIDTH), jnp.bfloat16)
    state = pltpu.VMEM((BLOCK_STATES // LANES, SSM_ROWS, LANES), jnp.float32)
    assert 2 * BLOCK_STATES == ROW_WIDTH
    return pl.pallas_call(
        kern,
        grid=(nblk, r // SSM_ROWS),
        in_specs=[pl.BlockSpec((1, SSM_ROWS, size), lambda g, i: (g, i, 0)),
                  pl.BlockSpec((CHUNK, SSM_GROUP, LANES), lambda g, i: (0, 0, g)),
                  pl.BlockSpec((CHUNK, SSM_GROUP, BLOCK_STATES), lambda g, i: (0, 0, g)),
                  pl.BlockSpec((CHUNK, SSM_GROUP, BLOCK_STATES), lambda g, i: (0, 0, g)),
                  pl.BlockSpec((CHUNK, SSM_STATE, LANES), lambda g, i: (0, 0, g)),
                  pl.BlockSpec((CHUNK, SSM_STATE, LANES), lambda g, i: (0, 0, g)),
                  pl.BlockSpec((1,) + ap.shape[1:], lambda g, i: (g, 0, 0)),
                  pl.BlockSpec((1,) + aq.shape[1:], lambda g, i: (g, 0, 0)),
                  pl.BlockSpec((wrows, wblk), lambda g, i: (0, g * nb + i)),
                  pl.BlockSpec((1, wblk), lambda g, i: (0, g * nb + i))],
        out_specs=[pl.BlockSpec((SSM_ROWS * CHUNK, LANES), lambda g, i: (i, g)),
                   pl.BlockSpec((wrows, wblk), lambda g, i: (0, g * nb + i))],
        out_shape=[jax.ShapeDtypeStruct((r * CHUNK, nblk * LANES), jnp.float32),
                   jax.ShapeDtypeStruct((wrows, wcols), jnp.bfloat16)],
        scratch_shapes=[mat, mat, mat,
                        pltpu.VMEM((1, BLOCK_STATES), jnp.float32),
                        pltpu.VMEM((1, BLOCK_STATES), jnp.float32),
                        state, state, state, state],
        compiler_params=pltpu.CompilerParams(dimension_semantics=("arbitrary", "arbitrary"),
                                             vmem_limit_bytes=VMEM_LIMIT),
        name="ssm_chunk",
    )(u4, kb, *fb, *mb, ap, aq, w_f32, w_scale)


def _main_kernel(x_ref, hin_ref, y_ref, w_ref, cw_ref, cb_ref, lng_ref, lnb_ref, wco_ref,
                 wglu_ref, bglu_ref, wso_ref, wout_ref, post_g_ref, o_ref,
                 glu_ref, cs_ref, cv_ref, pz_ref, so_ref, a_ref, acc_ref, yg_ref, yb_ref,
                 lin_ref, *, tile, d, ds):
    h_ref = hin_ref.at[0]
    bf16, f32 = jnp.bfloat16, jnp.float32
    o_zc = 2 * d
    p_zs, p_gc, p_gs, p_end = d, d + ds, 2 * d + ds, 3 * d + ds
    nlane = d // LANES
    pieces = [slice(r, r + EW_ROWS) for r in range(0, tile, EW_ROWS)]

    @pl.when(pl.program_id(1) == 0)
    def _():
        cs_ref[:, 0:CONV_HALO, :] = jnp.zeros((nlane, CONV_HALO, LANES), f32)

    glu_ref[...] = jnp.dot(h_ref[...], w_ref[:, 0:o_zc], preferred_element_type=f32)
    for rows in pieces:
        cu = glu_ref[rows, 0:d] * _sigmoid(glu_ref[rows, d:2 * d])
        for j in range(nlane):
            cs_ref[j, CONV_HALO + rows.start:CONV_HALO + rows.stop, :] = cu[:, j * LANES:(j + 1) * LANES]

    def ssm_glu():
        for rows in pieces:
            yg = _gelu_tanh(y_ref[0, rows, :])
            yg_ref[rows, :] = yg
            yb_ref[rows, :] = yg.astype(bf16)
        lin_ref[...] = jnp.dot(yb_ref[...], wglu_ref[...], preferred_element_type=f32)

    def ssm_out():
        for rows in pieces:
            gate = _sigmoid(lin_ref[rows, :] + bglu_ref[...]) * _silu(pz_ref[rows, p_zs:p_gc])
            yb_ref[rows, :] = (yg_ref[rows, :] * gate).astype(bf16)
        so_ref[...] = jnp.dot(yb_ref[...], wso_ref[...], preferred_element_type=f32)

    def conv_lane_block(j, lanes):
        for r in range(tile // CONV_ROWS):
            acc = jnp.broadcast_to(cb_ref[:, lanes], (CONV_ROWS, LANES))
            for lag in range(CONV_SIZE):
                lo = CONV_HALO + r * CONV_ROWS - lag
                tap = cw_ref[CONV_SIZE - 1 - lag:CONV_SIZE - lag, lanes]
                acc = acc + tap * cs_ref[j, lo:lo + CONV_ROWS, :]
            cv_ref[r * CONV_ROWS:(r + 1) * CONV_ROWS, lanes] = acc

    pz_ref[:, 0:p_zs] = jnp.dot(h_ref[...], w_ref[:, o_zc:o_zc + p_zs], preferred_element_type=f32)
    pz_ref[:, p_zs:p_end] = jnp.dot(h_ref[...], w_ref[:, o_zc + p_zs + ds:o_zc + p_end + ds],
                                    preferred_element_type=f32)
    ssm_glu()
    ssm_out()

    def conv_step(j, carry):
        conv_lane_block(j, pl.ds(pl.multiple_of(j * LANES, LANES), LANES))
        return carry

    lax.fori_loop(0, nlane, conv_step, 0)
    cs_ref[:, 0:CONV_HALO, :] = cs_ref[:, tile:CONV_HALO + tile, :]

    for rows in pieces:
        cv = cv_ref[rows, :]
        xc = cv - jnp.mean(cv, axis=-1, keepdims=True)
        var = jnp.mean(xc * xc, axis=-1, keepdims=True)
        ln = xc * lax.rsqrt(var + LN_EPS) * lng_ref[...] + lnb_ref[...]
        a_ref[rows, :] = (_silu(ln) * _silu(pz_ref[rows, 0:p_zs])).astype(bf16)
    acc_ref[...] = jnp.dot(a_ref[...], wco_ref[...], preferred_element_type=f32)

    for rows in pieces:
        merged = (_sigmoid(pz_ref[rows, p_gc:p_gs]) * acc_ref[rows, :]
                  + _sigmoid(pz_ref[rows, p_gs:p_end]) * so_ref[rows, :])
        a_ref[rows, :] = merged.astype(bf16)
    acc_ref[...] = jnp.dot(a_ref[...], wout_ref[...], preferred_element_type=f32)
    for rows in pieces:
        o_ref[0, rows, :] = x_ref[0, rows, :] + _rms_scale(acc_ref[rows, :]) * post_g_ref[...]


def _main(x, h, y, w_in, cw, cb, lng, lnb, wco, wglu, bglu, wso, wout, post_g):
    b, l, d = x.shape
    ds = y.shape[-1]
    tile = MAIN_TILE
    assert CONV_HALO >= CONV_SIZE - 1 and CONV_HALO % SUBLANES == 0 and tile >= CONV_HALO
    kern = functools.partial(_main_kernel, tile=tile, d=d, ds=ds)
    consts = (w_in, cw, cb, lng, lnb, wco, wglu, bglu, wso, wout, post_g)
    return pl.pallas_call(
        kern,
        grid=(b, l // tile),
        in_specs=[pl.BlockSpec((1, tile, d), lambda i, j: (i, j, 0)),
                  pl.BlockSpec((1, tile, d), lambda i, j: (i, j, 0)),
                  pl.BlockSpec((1, tile, ds), lambda i, j: (i, j, 0))]
                 + [_const_spec(a.shape) for a in consts],
        out_specs=pl.BlockSpec((1, tile, d), lambda i, j: (i, j, 0)),
        out_shape=jax.ShapeDtypeStruct((b, l, d), x.dtype),
        scratch_shapes=[pltpu.VMEM((tile, 2 * d), jnp.float32),
                        pltpu.VMEM((d // LANES, CONV_HALO + tile, LANES), jnp.float32),
                        pltpu.VMEM((tile, d), jnp.float32),
                        pltpu.VMEM((tile, 3 * d + ds), jnp.float32),
                        pltpu.VMEM((tile, d), jnp.float32),
                        pltpu.VMEM((tile, d), jnp.bfloat16),
                        pltpu.VMEM((tile, d), jnp.float32),
                        pltpu.VMEM((tile, ds), jnp.float32),
                        pltpu.VMEM((tile, ds), jnp.bfloat16),
                        pltpu.VMEM((tile, ds), jnp.float32)],
        compiler_params=pltpu.CompilerParams(dimension_semantics=("arbitrary", "arbitrary"),
                                             vmem_limit_bytes=VMEM_LIMIT),
        name="main_block",
    )(x, h, y, *consts)


def _layer(x, pre_g, w_in, conv_w, conv_b, ln_g, ln_b, w_conv_out, lam_re, lam_im, log_dt,
           b_re, b_im, c_re, c_im, d_skip, w_glu, b_glu, w_ssm_out, w_out, post_g):
    bsz, length, d = x.shape
    ds = w_glu.shape[0]
    bf16 = jnp.bfloat16
    o_us = 3 * d
    col = jnp.arange(w_in.shape[1])
    half_cols = ~((col < d) | ((col >= o_us) & (col < o_us + ds)))
    w_scale = jnp.where(half_cols, 0.5, 1.0).astype(w_in.dtype)
    row = lambda v: v.reshape(1, -1)
    assert lam_re.shape == (ds // SSM_GROUP, SSM_STATE)

    u4, h = _us_proj(x.reshape(bsz * length, d), row(pre_g), w_in, o_us, ds)
    n_levels = SSM_ROWS.bit_length() - 1
    kb, fb, mb, ap, aq = _ssm_compact(lam_re, lam_im, log_dt, b_re, b_im, c_re, c_im, d_skip, n_levels)
    y, w_bf = _ssm_chunk(u4, kb, fb, mb, ap, aq, w_in, row(w_scale), length // CHUNK)
    y = y.reshape(bsz, length, ds)

    cw = jnp.concatenate([conv_w, jnp.zeros((SUBLANES - CONV_SIZE % SUBLANES, d), conv_w.dtype)], axis=0)
    return _main(x, h.reshape(bsz, length, d), y, w_bf, cw, row(conv_b), row(0.5 * ln_g), row(0.5 * ln_b),
                 w_conv_out.astype(bf16), (0.5 * w_glu).astype(bf16), row(0.5 * b_glu),
                 w_ssm_out.astype(bf16), w_out.astype(bf16), row(post_g))


def kernel(x, pre_norm_gain, w_in, conv_w, conv_b, conv_ln_gain, conv_ln_bias, w_conv_out, ssm_lambda_re, ssm_lambda_im, ssm_log_dt, ssm_b_re, ssm_b_im, ssm_c_re, ssm_c_im, ssm_d, w_ssm_glu, b_ssm_glu, w_ssm_out, w_out, post_norm_gain):
    for l in range(pre_norm_gain.shape[0]):
        x = _layer(x, pre_norm_gain[l], w_in[l], conv_w[l], conv_b[l], conv_ln_gain[l],
                   conv_ln_bias[l], w_conv_out[l], ssm_lambda_re[l], ssm_lambda_im[l],
                   ssm_log_dt[l], ssm_b_re[l], ssm_b_im[l], ssm_c_re[l], ssm_c_im[l], ssm_d[l],
                   w_ssm_glu[l], b_ssm_glu[l], w_ssm_out[l], w_out[l], post_norm_gain[l])
    return x
```

```python
import functools
import math

import jax
import jax.numpy as jnp
from jax import lax
from jax.experimental import pallas as pl
from jax.experimental.pallas import tpu as pltpu

RMS_EPS = 1e-6
LN_EPS = 1e-5
CONV_SIZE = 31
SSM_GROUP = 16
SSM_STATE = 64
CHUNK = 8
LANES = 128
SUBLANES = 8
GROUPS_PER_BLOCK = LANES // SSM_GROUP
BLOCK_STATES = GROUPS_PER_BLOCK * SSM_STATE
ROW_WIDTH = CHUNK * LANES

US_TILE = 2048
SSM_ROWS = 512
MAIN_TILE = 512
CONV_HALO = 32
CONV_ROWS = 64
MXU_COLS = 256
EW_ROWS = 16
VMEM_LIMIT = 56 * 1024 * 1024


def _sigmoid(h):
    return 0.5 * jnp.tanh(h) + 0.5


def _silu(h):
    return h * jnp.tanh(h) + h


def _gelu_tanh(x):
    c = math.sqrt(2.0 / math.pi)
    return 0.5 * x * (1.0 + jnp.tanh(c * (x + 0.044715 * (x * x * x))))


def _rms_scale(x):
    return x * lax.rsqrt(jnp.mean(x * x, axis=-1, keepdims=True) + RMS_EPS)


def _const_spec(shape):
    zeros = (0,) * len(shape)
    return pl.BlockSpec(shape, lambda *_: zeros)


def _us_proj_kernel(x_ref, g_ref, w_ref, u_ref, slab_ref, *, tile, nblk):
    h = _rms_scale(x_ref[...]) * g_ref[...]
    u = jnp.dot(h.astype(jnp.bfloat16), w_ref[...].astype(jnp.bfloat16),
                preferred_element_type=jnp.float32)
    for b in range(nblk):
        slab_ref[b] = u[:, b * LANES:(b + 1) * LANES]
    for b in range(nblk):
        for k in range(CHUNK):
            u_ref[b, :, k * LANES:(k + 1) * LANES] = slab_ref[
                b, pl.ds(k, tile // CHUNK, stride=CHUNK), :].astype(u_ref.dtype)


def _us_proj(x2, pre_g, w_in, col, n):
    t, d = x2.shape
    nblk = n // LANES
    assert col % n == 0
    kern = functools.partial(_us_proj_kernel, tile=US_TILE, nblk=nblk)
    return pl.pallas_call(
        kern,
        grid=(t // US_TILE,),
        in_specs=[pl.BlockSpec((US_TILE, d), lambda i: (i, 0)),
                  _const_spec((1, d)),
                  pl.BlockSpec((d, n), lambda i: (0, col // n))],
        out_specs=pl.BlockSpec((nblk, US_TILE // CHUNK, ROW_WIDTH), lambda i: (0, i, 0)),
        out_shape=jax.ShapeDtypeStruct((nblk, t // CHUNK, ROW_WIDTH), jnp.bfloat16),
        scratch_shapes=[pltpu.VMEM((nblk, US_TILE, LANES), jnp.float32)],
        compiler_params=pltpu.CompilerParams(dimension_semantics=("arbitrary",),
                                             vmem_limit_bytes=VMEM_LIMIT),
        name="us_proj",
    )(x2, pre_g, w_in)


def _ssm_compact(lam_re, lam_im, log_dt, b_re, b_im, c_re, c_im, d, n_levels):
    groups, states = lam_re.shape
    gh = groups * SSM_GROUP
    dt = jnp.exp(log_dt)[:, None]
    lr, li = lam_re, lam_im
    scan_pows = [CHUNK << lev for lev in range(n_levels)]
    row_pows = [CHUNK * r for r in range(SUBLANES)]
    nv = jnp.asarray(list(range(CHUNK + 1)) + scan_pows + row_pows, jnp.float32)[:, None, None]
    mag = jnp.exp(nv * (lr * dt))
    pw_r = mag * jnp.cos(nv * (li * dt))
    pw_i = mag * jnp.sin(nv * (li * dt))
    pr, pi = pw_r[:CHUNK + 1], pw_i[:CHUNK + 1]
    ar, ai = pr[1], pi[1]
    den = lr * lr + li * li
    zr = ((ar - 1.0) * lr + ai * li) / den
    zi = (ai * lr - (ar - 1.0) * li) / den
    bbr = zr[..., None] * b_re - zi[..., None] * b_im
    bbi = zr[..., None] * b_im + zi[..., None] * b_re

    bt_r = bbr.transpose(2, 0, 1).reshape(1, SSM_GROUP, groups * states)
    bt_i = bbi.transpose(2, 0, 1).reshape(1, SSM_GROUP, groups * states)
    qr = pr[CHUNK - 1::-1].reshape(CHUNK, 1, groups * states)
    qi = pi[CHUNK - 1::-1].reshape(CHUNK, 1, groups * states)
    fb = (qr * bt_r - qi * bt_i, qr * bt_i + qi * bt_r)

    spread = lambda v: jnp.repeat(jnp.swapaxes(v, -1, -2), SSM_GROUP, axis=-1)
    px_r, px_i = spread(pr), spread(pi)
    ct_r = c_re.transpose(2, 0, 1).reshape(1, states, gh)
    ct_i = c_im.transpose(2, 0, 1).reshape(1, states, gh)
    m_r = ct_r * px_r - ct_i * px_i
    m_i = ct_r * px_i + ct_i * px_r
    mb = (m_r[1:], -m_i[1:])
    bx_r = spread(bbr.transpose(2, 0, 1))
    bx_i = spread(bbi.transpose(2, 0, 1))
    kb = jnp.sum(m_r[:CHUNK, None] * bx_r[None] - m_i[:CHUNK, None] * bx_i[None], axis=2)
    skip = (d.T[:, :, None] * jnp.eye(SSM_GROUP, dtype=d.dtype)[:, None, :]).reshape(SSM_GROUP, gh)
    kb = kb.at[0].add(skip)

    nblk = groups // GROUPS_PER_BLOCK
    lo, hi = CHUNK + 1, CHUNK + 1 + n_levels
    ap = jnp.stack([pw_r[lo:hi], pw_i[lo:hi]], axis=1)
    ap = ap.reshape(2 * n_levels, nblk, BLOCK_STATES).transpose(1, 0, 2)
    aq = jnp.concatenate([pw_r[hi:], pw_i[hi:]])
    aq = aq.reshape(2 * SUBLANES, nblk, BLOCK_STATES).transpose(1, 0, 2)
    return kb, fb, mb, ap, aq


def _block_diag(compact, group_rows, group_cols):
    n = compact.shape[1]
    tiled = jnp.concatenate([compact] * GROUPS_PER_BLOCK, axis=0)
    r = lax.broadcasted_iota(jnp.int32, tiled.shape, 0) // group_rows
    c = lax.broadcasted_iota(jnp.int32, tiled.shape, 1) // group_cols
    return jnp.where(r == c, tiled, 0.0).astype(jnp.bfloat16)


def _ssm_chunk_kernel(*refs, rows, n_levels, blocks_per_seq, cast_scales):
    n_cast = len(cast_scales)
    (u_ref, kb_ref, fbr_ref, fbi_ref, mbr_ref, mbi_ref, ap_ref, aq_ref, wf_ref, ws_ref) = refs[:10]
    cast_in = refs[10:10 + n_cast]
    y_ref, wbf_ref = refs[10 + n_cast:12 + n_cast]
    cast_out = refs[12 + n_cast:12 + 2 * n_cast]
    (tz_ref, wb_ref, wc_ref, cre_ref, cim_ref, lre_ref, lim_ref, ebr_ref, ebi_ref) = refs[12 + 2 * n_cast:]
    half = BLOCK_STATES
    wbf_ref[...] = (wf_ref[...] * ws_ref[...]).astype(wbf_ref.dtype)
    for src, dst, scale in zip(cast_in, cast_out, cast_scales):
        dst[...] = (src[...] * scale).astype(dst.dtype)

    @pl.when(pl.program_id(1) == 0)
    def _():
        tz_ref[...] = jnp.zeros_like(tz_ref)
        for lag in range(CHUNK):
            blk = _block_diag(kb_ref[lag], SSM_GROUP, SSM_GROUP)
            for k in range(CHUNK - lag):
                t = k + lag
                tz_ref[k * LANES:(k + 1) * LANES, t * LANES:(t + 1) * LANES] = blk
        for ri, (fb_ref, mb_ref) in enumerate(((fbr_ref, mbr_ref), (fbi_ref, mbi_ref))):
            for k in range(CHUNK):
                wb_ref[k * LANES:(k + 1) * LANES, ri * half:(ri + 1) * half] = _block_diag(
                    fb_ref[k], SSM_GROUP, SSM_STATE)
                wc_ref[ri * half:(ri + 1) * half, k * LANES:(k + 1) * LANES] = _block_diag(
                    mb_ref[k], SSM_STATE, SSM_GROUP)

    @pl.when(pl.program_id(1) % blocks_per_seq == 0)
    def _():
        cre_ref[...] = jnp.zeros_like(cre_ref)
        cim_ref[...] = jnp.zeros_like(cim_ref)

    def scan_level(vr, vi, lev, shift, pos):
        mr, mi = ap_ref[0, 2 * lev:2 * lev + 1, :], ap_ref[0, 2 * lev + 1:2 * lev + 2, :]
        keep = pos >= shift
        sr = jnp.where(keep, pltpu.roll(vr, shift, 0), 0.0)
        si = jnp.where(keep, pltpu.roll(vi, shift, 0), 0.0)
        return vr + (mr * sr - mi * si), vi + (mr * si + mi * sr)

    u = u_ref[0]
    y1 = jnp.dot(u, tz_ref[...], preferred_element_type=jnp.float32)
    bc = jnp.dot(u, wb_ref[...], preferred_element_type=jnp.float32)
    n_grp = rows // SUBLANES
    in_levels = SUBLANES.bit_length() - 1
    assert n_grp == 1 << (n_levels - in_levels)

    tiles = lambda v: v.reshape(n_grp, SUBLANES, half)
    lre, lim = tiles(bc[:, :half]), tiles(bc[:, half:])
    rin = lax.broadcasted_iota(jnp.int32, (1, SUBLANES, 1), 1)
    for lev in range(in_levels):
        s = 1 << lev
        mr = jnp.where(rin >= s, ap_ref[0, 2 * lev:2 * lev + 1, :][None], 0.0)
        mi = jnp.where(rin >= s, ap_ref[0, 2 * lev + 1:2 * lev + 2, :][None], 0.0)
        sr, si = pltpu.roll(lre, s, 1), pltpu.roll(lim, s, 1)
        lre, lim = lre + (mr * sr - mi * si), lim + (mr * si + mi * sr)
    lsh_re = jnp.where(rin >= 1, pltpu.roll(lre, 1, 1), 0.0).reshape(rows, half)
    lsh_im = jnp.where(rin >= 1, pltpu.roll(lim, 1, 1), 0.0).reshape(rows, half)
    lre, lim = lre.reshape(rows, half), lim.reshape(rows, half)
    nslab = half // LANES
    slabs = [slice(q * LANES, (q + 1) * LANES) for q in range(nslab)]
    for q in range(nslab):
        lre_ref[q] = lre[:, slabs[q]]
        lim_ref[q] = lim[:, slabs[q]]

    ends = pl.ds(SUBLANES - 1, n_grp, stride=SUBLANES)
    ere = jnp.concatenate([lre_ref[q, ends, :] for q in range(nslab)], axis=1)
    eim = jnp.concatenate([lim_ref[q, ends, :] for q in range(nslab)], axis=1)
    grp = lax.broadcasted_iota(jnp.int32, (n_grp, 1), 0)
    cre, cim = cre_ref[...], cim_ref[...]
    gr, gi = ap_ref[0, 2 * in_levels:2 * in_levels + 1, :], ap_ref[0, 2 * in_levels + 1:2 * in_levels + 2, :]
    ere = ere + jnp.where(grp == 0, gr * cre - gi * cim, 0.0)
    eim = eim + jnp.where(grp == 0, gr * cim + gi * cre, 0.0)
    for lev in range(in_levels, n_levels):
        ere, eim = scan_level(ere, eim, lev, 1 << (lev - in_levels), grp)
    pre = jnp.where(grp == 0, cre, pltpu.roll(ere, 1, 0))
    pim = jnp.where(grp == 0, cim, pltpu.roll(eim, 1, 0))
    cre_ref[...] = ere[n_grp - 1:n_grp, :]
    cim_ref[...] = eim[n_grp - 1:n_grp, :]

    for q in range(nslab):
        for r in range(SUBLANES):
            ebr_ref[q, pl.ds(r, n_grp, stride=SUBLANES), :] = pre[:, slabs[q]]
            ebi_ref[q, pl.ds(r, n_grp, stride=SUBLANES), :] = pim[:, slabs[q]]
    ebr = jnp.concatenate([ebr_ref[q] for q in range(nslab)], axis=1)
    ebi = jnp.concatenate([ebi_ref[q] for q in range(nslab)], axis=1)
    qr = jnp.concatenate([aq_ref[0, 0:SUBLANES, :]] * n_grp, axis=0)
    qi = jnp.concatenate([aq_ref[0, SUBLANES:2 * SUBLANES, :]] * n_grp, axis=0)
    st_re = lsh_re + (qr * ebr - qi * ebi)
    st_im = lsh_im + (qr * ebi + qi * ebr)
    sst = jnp.concatenate([st_re, st_im], axis=1).astype(jnp.bfloat16)
    y = y1 + jnp.dot(sst, wc_ref[...], preferred_element_type=jnp.float32)
    for t in range(CHUNK):
        y_ref[pl.ds(t, rows, stride=CHUNK), :] = y[:, t * LANES:(t + 1) * LANES]


def _ssm_chunk(u4, kb, fb, mb, ap, aq, w_f32, w_scale, casts, rows_per_seq):
    nblk, r, size = u4.shape
    n_levels = ap.shape[1] // 2
    assert SSM_ROWS == 1 << n_levels and rows_per_seq % SSM_ROWS == 0 and size == ROW_WIDTH
    nb = r // SSM_ROWS
    steps = nblk * nb
    wrows, wcols = w_f32.shape
    wblk = wcols // steps
    assert wblk * steps == wcols and wblk % LANES == 0
    cast_specs = []
    for w, _ in casts:
        rblk = w.shape[0] // steps
        assert rblk * steps == w.shape[0] and rblk % (2 * SUBLANES) == 0
        cast_specs.append(pl.BlockSpec((rblk, w.shape[1]), lambda g, i: (g * nb + i, 0)))
    kern = functools.partial(_ssm_chunk_kernel, rows=SSM_ROWS, n_levels=n_levels,
                             blocks_per_seq=rows_per_seq // SSM_ROWS,
                             cast_scales=tuple(s for _, s in casts))
    mat = pltpu.VMEM((ROW_WIDTH, ROW_WIDTH), jnp.bfloat16)
    state = pltpu.VMEM((BLOCK_STATES // LANES, SSM_ROWS, LANES), jnp.float32)
    assert 2 * BLOCK_STATES == ROW_WIDTH
    return pl.pallas_call(
        kern,
        grid=(nblk, r // SSM_ROWS),
        in_specs=[pl.BlockSpec((1, SSM_ROWS, size), lambda g, i: (g, i, 0)),
                  pl.BlockSpec((CHUNK, SSM_GROUP, LANES), lambda g, i: (0, 0, g)),
                  pl.BlockSpec((CHUNK, SSM_GROUP, BLOCK_STATES), lambda g, i: (0, 0, g)),
                  pl.BlockSpec((CHUNK, SSM_GROUP, BLOCK_STATES), lambda g, i: (0, 0, g)),
                  pl.BlockSpec((CHUNK, SSM_STATE, LANES), lambda g, i: (0, 0, g)),
                  pl.BlockSpec((CHUNK, SSM_STATE, LANES), lambda g, i: (0, 0, g)),
                  pl.BlockSpec((1,) + ap.shape[1:], lambda g, i: (g, 0, 0)),
                  pl.BlockSpec((1,) + aq.shape[1:], lambda g, i: (g, 0, 0)),
                  pl.BlockSpec((wrows, wblk), lambda g, i: (0, g * nb + i)),
                  pl.BlockSpec((1, wblk), lambda g, i: (0, g * nb + i))] + cast_specs,
        out_specs=[pl.BlockSpec((SSM_ROWS * CHUNK, LANES), lambda g, i: (i, g)),
                   pl.BlockSpec((wrows, wblk), lambda g, i: (0, g * nb + i))] + cast_specs,
        out_shape=[jax.ShapeDtypeStruct((r * CHUNK, nblk * LANES), jnp.float32),
                   jax.ShapeDtypeStruct((wrows, wcols), jnp.bfloat16)]
                  + [jax.ShapeDtypeStruct(w.shape, jnp.bfloat16) for w, _ in casts],
        scratch_shapes=[mat, mat, mat,
                        pltpu.VMEM((1, BLOCK_STATES), jnp.float32),
                        pltpu.VMEM((1, BLOCK_STATES), jnp.float32),
                        state, state, state, state],
        compiler_params=pltpu.CompilerParams(dimension_semantics=("arbitrary", "arbitrary"),
                                             vmem_limit_bytes=VMEM_LIMIT),
        name="ssm_chunk",
    )(u4, kb, *fb, *mb, ap, aq, w_f32, w_scale, *[w for w, _ in casts])


def _main_kernel(x_ref, y_ref, pre_g_ref, w_ref, cw_ref, cb_ref, lng_ref, lnb_ref, wco_ref,
                 wglu_ref, bglu_ref, wso_ref, wout_ref, post_g_ref, o_ref,
                 h_ref, glu_ref, cs_ref, cv_ref, pz_ref, so_ref, a_ref, acc_ref, yg_ref, yb_ref,
                 lin_ref, *, tile, d, ds):
    bf16, f32 = jnp.bfloat16, jnp.float32
    o_zc = 2 * d
    p_zs, p_gc, p_gs, p_end = d, d + ds, 2 * d + ds, 3 * d + ds
    nlane = d // LANES
    pieces = [slice(r, r + EW_ROWS) for r in range(0, tile, EW_ROWS)]

    @pl.when(pl.program_id(1) == 0)
    def _():
        cs_ref[:, 0:CONV_HALO, :] = jnp.zeros((nlane, CONV_HALO, LANES), f32)

    for rows in pieces:
        h_ref[rows, :] = (_rms_scale(x_ref[0, rows, :]) * pre_g_ref[...]).astype(bf16)

    glu_ref[...] = jnp.dot(h_ref[...], w_ref[:, 0:o_zc], preferred_element_type=f32)
    for rows in pieces:
        cu = glu_ref[rows, 0:d] * _sigmoid(glu_ref[rows, d:2 * d])
        for j in range(nlane):
            cs_ref[j, CONV_HALO + rows.start:CONV_HALO + rows.stop, :] = cu[:, j * LANES:(j + 1) * LANES]

    def ssm_glu():
        for rows in pieces:
            yg = _gelu_tanh(y_ref[0, rows, :])
            yg_ref[rows, :] = yg
            yb_ref[rows, :] = yg.astype(bf16)
        lin_ref[...] = jnp.dot(yb_ref[...], wglu_ref[...], preferred_element_type=f32)

    def ssm_out():
        for rows in pieces:
            gate = _sigmoid(lin_ref[rows, :] + bglu_ref[...]) * _silu(pz_ref[rows, p_zs:p_gc])
            yb_ref[rows, :] = (yg_ref[rows, :] * gate).astype(bf16)
        so_ref[...] = jnp.dot(yb_ref[...], wso_ref[...], preferred_element_type=f32)

    def conv_lane_block(j, lanes):
        for r in range(tile // CONV_ROWS):
            acc = jnp.broadcast_to(cb_ref[:, lanes], (CONV_ROWS, LANES))
            for lag in range(CONV_SIZE):
                lo = CONV_HALO + r * CONV_ROWS - lag
                tap = cw_ref[CONV_SIZE - 1 - lag:CONV_SIZE - lag, lanes]
                acc = acc + tap * cs_ref[j, lo:lo + CONV_ROWS, :]
            cv_ref[r * CONV_ROWS:(r + 1) * CONV_ROWS, lanes] = acc

    pz_ref[:, 0:p_zs] = jnp.dot(h_ref[...], w_ref[:, o_zc:o_zc + p_zs], preferred_element_type=f32)
    pz_ref[:, p_zs:p_end] = jnp.dot(h_ref[...], w_ref[:, o_zc + p_zs + ds:o_zc + p_end + ds],
                                    preferred_element_type=f32)
    ssm_glu()
    ssm_out()

    def conv_step(j, carry):
        conv_lane_block(j, pl.ds(pl.multiple_of(j * LANES, LANES), LANES))
        return carry

    lax.fori_loop(0, nlane, conv_step, 0)
    cs_ref[:, 0:CONV_HALO, :] = cs_ref[:, tile:CONV_HALO + tile, :]

    for rows in pieces:
        cv = cv_ref[rows, :]
        xc = cv - jnp.mean(cv, axis=-1, keepdims=True)
        var = jnp.mean(xc * xc, axis=-1, keepdims=True)
        ln = xc * lax.rsqrt(var + LN_EPS) * lng_ref[...] + lnb_ref[...]
        a_ref[rows, :] = (_silu(ln) * _silu(pz_ref[rows, 0:p_zs])).astype(bf16)
    acc_ref[...] = jnp.dot(a_ref[...], wco_ref[...], preferred_element_type=f32)

    for rows in pieces:
        merged = (_sigmoid(pz_ref[rows, p_gc:p_gs]) * acc_ref[rows, :]
                  + _sigmoid(pz_ref[rows, p_gs:p_end]) * so_ref[rows, :])
        a_ref[rows, :] = merged.astype(bf16)
    acc_ref[...] = jnp.dot(a_ref[...], wout_ref[...], preferred_element_type=f32)
    for rows in pieces:
        o_ref[0, rows, :] = x_ref[0, rows, :] + _rms_scale(acc_ref[rows, :]) * post_g_ref[...]


def _main(x, y, pre_g, w_in, cw, cb, lng, lnb, wco, wglu, bglu, wso, wout, post_g):
    b, l, d = x.shape
    ds = y.shape[-1]
    tile = MAIN_TILE
    assert CONV_HALO >= CONV_SIZE - 1 and CONV_HALO % SUBLANES == 0 and tile >= CONV_HALO
    kern = functools.partial(_main_kernel, tile=tile, d=d, ds=ds)
    consts = (pre_g, w_in, cw, cb, lng, lnb, wco, wglu, bglu, wso, wout, post_g)
    return pl.pallas_call(
        kern,
        grid=(b, l // tile),
        in_specs=[pl.BlockSpec((1, tile, d), lambda i, j: (i, j, 0)),
                  pl.BlockSpec((1, tile, ds), lambda i, j: (i, j, 0))]
                 + [_const_spec(a.shape) for a in consts],
        out_specs=pl.BlockSpec((1, tile, d), lambda i, j: (i, j, 0)),
        out_shape=jax.ShapeDtypeStruct((b, l, d), x.dtype),
        scratch_shapes=[pltpu.VMEM((tile, d), jnp.bfloat16),
                        pltpu.VMEM((tile, 2 * d), jnp.float32),
                        pltpu.VMEM((d // LANES, CONV_HALO + tile, LANES), jnp.float32),
                        pltpu.VMEM((tile, d), jnp.float32),
                        pltpu.VMEM((tile, 3 * d + ds), jnp.float32),
                        pltpu.VMEM((tile, d), jnp.float32),
                        pltpu.VMEM((tile, d), jnp.bfloat16),
                        pltpu.VMEM((tile, d), jnp.float32),
                        pltpu.VMEM((tile, ds), jnp.float32),
                        pltpu.VMEM((tile, ds), jnp.bfloat16),
                        pltpu.VMEM((tile, ds), jnp.float32)],
        compiler_params=pltpu.CompilerParams(dimension_semantics=("arbitrary", "arbitrary"),
                                             vmem_limit_bytes=VMEM_LIMIT),
        name="main_block",
    )(x, y, *consts)


def _layer(x, pre_g, w_in, conv_w, conv_b, ln_g, ln_b, w_conv_out, lam_re, lam_im, log_dt,
           b_re, b_im, c_re, c_im, d_skip, w_glu, b_glu, w_ssm_out, w_out, post_g):
    bsz, length, d = x.shape
    ds = w_glu.shape[0]
    o_us = 3 * d
    col = jnp.arange(w_in.shape[1])
    half_cols = ~((col < d) | ((col >= o_us) & (col < o_us + ds)))
    w_scale = jnp.where(half_cols, 0.5, 1.0).astype(w_in.dtype)
    row = lambda v: v.reshape(1, -1)
    assert lam_re.shape == (ds // SSM_GROUP, SSM_STATE)

    u4 = _us_proj(x.reshape(bsz * length, d), row(pre_g), w_in, o_us, ds)
    n_levels = SSM_ROWS.bit_length() - 1
    kb, fb, mb, ap, aq = _ssm_compact(lam_re, lam_im, log_dt, b_re, b_im, c_re, c_im, d_skip, n_levels)
    casts = [(w_conv_out, 1.0), (w_glu, 0.5), (w_ssm_out, 1.0), (w_out, 1.0)]
    y, w_bf, wco_bf, wglu_bf, wso_bf, wout_bf = _ssm_chunk(
        u4, kb, fb, mb, ap, aq, w_in, row(w_scale), casts, length // CHUNK)
    y = y.reshape(bsz, length, ds)

    cw = jnp.concatenate([conv_w, jnp.zeros((SUBLANES - CONV_SIZE % SUBLANES, d), conv_w.dtype)], axis=0)
    return _main(x, y, row(pre_g), w_bf, cw, row(conv_b), row(0.5 * ln_g), row(0.5 * ln_b),
                 wco_bf, wglu_bf, row(0.5 * b_glu), wso_bf, wout_bf, row(post_g))


def kernel(x, pre_norm_gain, w_in, conv_w, conv_b, conv_ln_gain, conv_ln_bias, w_conv_out, ssm_lambda_re, ssm_lambda_im, ssm_log_dt, ssm_b_re, ssm_b_im, ssm_c_re, ssm_c_im, ssm_d, w_ssm_glu, b_ssm_glu, w_ssm_out, w_out, post_norm_gain):
    for l in range(pre_norm_gain.shape[0]):
        x = _layer(x, pre_norm_gain[l], w_in[l], conv_w[l], conv_b[l], conv_ln_gain[l],
                   conv_ln_bias[l], w_conv_out[l], ssm_lambda_re[l], ssm_lambda_im[l],
                   ssm_log_dt[l], ssm_b_re[l], ssm_b_im[l], ssm_c_re[l], ssm_c_im[l], ssm_d[l],
                   w_ssm_glu[l], b_ssm_glu[l], w_ssm_out[l], w_out[l], post_norm_gain[l])
    return x
```

```python
import functools
import math

import jax
import jax.numpy as jnp
from jax import lax
from jax.experimental import pallas as pl
from jax.experimental.pallas import tpu as pltpu

RMS_EPS = 1e-6
LN_EPS = 1e-5
CONV_SIZE = 31
SSM_GROUP = 16
SSM_STATE = 64
CHUNK = 8
LANES = 128
SUBLANES = 8
GROUPS_PER_BLOCK = LANES // SSM_GROUP
BLOCK_STATES = GROUPS_PER_BLOCK * SSM_STATE
ROW_WIDTH = CHUNK * LANES

US_TILE = 2048
SSM_ROWS = 512
MAIN_TILE = 512
CONV_HALO = 32
CONV_ROWS = 64
MXU_COLS = 256
EW_ROWS = 16
VMEM_LIMIT = 56 * 1024 * 1024


def _sigmoid(h):
    return 0.5 * jnp.tanh(h) + 0.5


def _silu(h):
    return h * jnp.tanh(h) + h


def _gelu_tanh(x):
    c = math.sqrt(2.0 / math.pi)
    return 0.5 * x * (1.0 + jnp.tanh(c * (x + 0.044715 * (x * x * x))))


def _rms_scale(x):
    return x * lax.rsqrt(jnp.mean(x * x, axis=-1, keepdims=True) + RMS_EPS)


def _const_spec(shape):
    zeros = (0,) * len(shape)
    return pl.BlockSpec(shape, lambda *_: zeros)


def _us_proj_kernel(x_ref, g_ref, w_ref, u_ref, slab_ref, *, tile, nblk):
    h = _rms_scale(x_ref[...]) * g_ref[...]
    u = jnp.dot(h.astype(jnp.bfloat16), w_ref[...].astype(jnp.bfloat16),
                preferred_element_type=jnp.float32)
    for b in range(nblk):
        slab_ref[b] = u[:, b * LANES:(b + 1) * LANES]
    for b in range(nblk):
        for k in range(CHUNK):
            u_ref[b, :, k * LANES:(k + 1) * LANES] = slab_ref[
                b, pl.ds(k, tile // CHUNK, stride=CHUNK), :].astype(u_ref.dtype)


def _us_proj(x2, pre_g, w_in, col, n):
    t, d = x2.shape
    nblk = n // LANES
    assert col % n == 0
    kern = functools.partial(_us_proj_kernel, tile=US_TILE, nblk=nblk)
    return pl.pallas_call(
        kern,
        grid=(t // US_TILE,),
        in_specs=[pl.BlockSpec((US_TILE, d), lambda i: (i, 0)),
                  _const_spec((1, d)),
                  pl.BlockSpec((d, n), lambda i: (0, col // n))],
        out_specs=pl.BlockSpec((nblk, US_TILE // CHUNK, ROW_WIDTH), lambda i: (0, i, 0)),
        out_shape=jax.ShapeDtypeStruct((nblk, t // CHUNK, ROW_WIDTH), jnp.bfloat16),
        scratch_shapes=[pltpu.VMEM((nblk, US_TILE, LANES), jnp.float32)],
        compiler_params=pltpu.CompilerParams(dimension_semantics=("arbitrary",),
                                             vmem_limit_bytes=VMEM_LIMIT),
        name="us_proj",
    )(x2, pre_g, w_in)


def _ssm_compact(lam_re, lam_im, log_dt, b_re, b_im, c_re, c_im, d, n_levels):
    groups, states = lam_re.shape
    gh = groups * SSM_GROUP
    dt = jnp.exp(log_dt)[:, None]
    lr, li = lam_re, lam_im
    scan_pows = [CHUNK << lev for lev in range(n_levels)]
    row_pows = [CHUNK * r for r in range(SUBLANES)]
    nv = jnp.asarray(list(range(CHUNK + 1)) + scan_pows + row_pows, jnp.float32)[:, None, None]
    mag = jnp.exp(nv * (lr * dt))
    pw_r = mag * jnp.cos(nv * (li * dt))
    pw_i = mag * jnp.sin(nv * (li * dt))
    pr, pi = pw_r[:CHUNK + 1], pw_i[:CHUNK + 1]
    ar, ai = pr[1], pi[1]
    den = lr * lr + li * li
    zr = ((ar - 1.0) * lr + ai * li) / den
    zi = (ai * lr - (ar - 1.0) * li) / den
    bbr = zr[..., None] * b_re - zi[..., None] * b_im
    bbi = zr[..., None] * b_im + zi[..., None] * b_re

    bt_r = bbr.transpose(2, 0, 1).reshape(1, SSM_GROUP, groups * states)
    bt_i = bbi.transpose(2, 0, 1).reshape(1, SSM_GROUP, groups * states)
    qr = pr[CHUNK - 1::-1].reshape(CHUNK, 1, groups * states)
    qi = pi[CHUNK - 1::-1].reshape(CHUNK, 1, groups * states)
    fb = (qr * bt_r - qi * bt_i, qr * bt_i + qi * bt_r)

    spread = lambda v: jnp.repeat(jnp.swapaxes(v, -1, -2), SSM_GROUP, axis=-1)
    px_r, px_i = spread(pr), spread(pi)
    ct_r = c_re.transpose(2, 0, 1).reshape(1, states, gh)
    ct_i = c_im.transpose(2, 0, 1).reshape(1, states, gh)
    m_r = ct_r * px_r - ct_i * px_i
    m_i = ct_r * px_i + ct_i * px_r
    mb = (m_r[1:], -m_i[1:])
    er = pr[:CHUNK, :, :, None] * bbr - pi[:CHUNK, :, :, None] * bbi
    ei = pr[:CHUNK, :, :, None] * bbi + pi[:CHUNK, :, :, None] * bbr
    hp = lax.Precision.HIGHEST
    kj = (jnp.einsum("gxp,jgpi->jgxi", c_re, er, precision=hp)
          - jnp.einsum("gxp,jgpi->jgxi", c_im, ei, precision=hp))
    kb = kj.transpose(0, 3, 1, 2).reshape(CHUNK, SSM_GROUP, gh)
    skip = (d.T[:, :, None] * jnp.eye(SSM_GROUP, dtype=d.dtype)[:, None, :]).reshape(SSM_GROUP, gh)
    kb = kb.at[0].add(skip)

    nblk = groups // GROUPS_PER_BLOCK
    lo, hi = CHUNK + 1, CHUNK + 1 + n_levels
    ap = jnp.stack([pw_r[lo:hi], pw_i[lo:hi]], axis=1)
    ap = ap.reshape(2 * n_levels, nblk, BLOCK_STATES).transpose(1, 0, 2)
    aq = jnp.concatenate([pw_r[hi:], pw_i[hi:]])
    aq = aq.reshape(2 * SUBLANES, nblk, BLOCK_STATES).transpose(1, 0, 2)
    return kb, fb, mb, ap, aq


def _block_diag(compact, group_rows, group_cols):
    n = compact.shape[1]
    tiled = jnp.concatenate([compact] * GROUPS_PER_BLOCK, axis=0)
    r = lax.broadcasted_iota(jnp.int32, tiled.shape, 0) // group_rows
    c = lax.broadcasted_iota(jnp.int32, tiled.shape, 1) // group_cols
    return jnp.where(r == c, tiled, 0.0).astype(jnp.bfloat16)


def _ssm_chunk_kernel(*refs, rows, n_levels, blocks_per_seq, cast_scales):
    n_cast = len(cast_scales)
    (u_ref, kb_ref, fbr_ref, fbi_ref, mbr_ref, mbi_ref, ap_ref, aq_ref, wf_ref, ws_ref) = refs[:10]
    cast_in = refs[10:10 + n_cast]
    y_ref, wbf_ref = refs[10 + n_cast:12 + n_cast]
    cast_out = refs[12 + n_cast:12 + 2 * n_cast]
    (tz_ref, wb_ref, wc_ref, cre_ref, cim_ref, lre_ref, lim_ref, ebr_ref, ebi_ref) = refs[12 + 2 * n_cast:]
    half = BLOCK_STATES
    wbf_ref[...] = (wf_ref[...] * ws_ref[...]).astype(wbf_ref.dtype)
    for src, dst, scale in zip(cast_in, cast_out, cast_scales):
        dst[...] = (src[...] * scale).astype(dst.dtype)

    @pl.when(pl.program_id(1) == 0)
    def _():
        tz_ref[...] = jnp.zeros_like(tz_ref)
        for lag in range(CHUNK):
            blk = _block_diag(kb_ref[lag], SSM_GROUP, SSM_GROUP)
            for k in range(CHUNK - lag):
                t = k + lag
                tz_ref[k * LANES:(k + 1) * LANES, t * LANES:(t + 1) * LANES] = blk
        for ri, (fb_ref, mb_ref) in enumerate(((fbr_ref, mbr_ref), (fbi_ref, mbi_ref))):
            for k in range(CHUNK):
                wb_ref[k * LANES:(k + 1) * LANES, ri * half:(ri + 1) * half] = _block_diag(
                    fb_ref[k], SSM_GROUP, SSM_STATE)
                wc_ref[ri * half:(ri + 1) * half, k * LANES:(k + 1) * LANES] = _block_diag(
                    mb_ref[k], SSM_STATE, SSM_GROUP)

    @pl.when(pl.program_id(1) % blocks_per_seq == 0)
    def _():
        cre_ref[...] = jnp.zeros_like(cre_ref)
        cim_ref[...] = jnp.zeros_like(cim_ref)

    def scan_level(vr, vi, lev, shift, pos):
        mr, mi = ap_ref[0, 2 * lev:2 * lev + 1, :], ap_ref[0, 2 * lev + 1:2 * lev + 2, :]
        keep = pos >= shift
        sr = jnp.where(keep, pltpu.roll(vr, shift, 0), 0.0)
        si = jnp.where(keep, pltpu.roll(vi, shift, 0), 0.0)
        return vr + (mr * sr - mi * si), vi + (mr * si + mi * sr)

    u = u_ref[0]
    y1 = jnp.dot(u, tz_ref[...], preferred_element_type=jnp.float32)
    bc = jnp.dot(u, wb_ref[...], preferred_element_type=jnp.float32)
    n_grp = rows // SUBLANES
    in_levels = SUBLANES.bit_length() - 1
    assert n_grp == 1 << (n_levels - in_levels)

    tiles = lambda v: v.reshape(n_grp, SUBLANES, half)
    lre, lim = tiles(bc[:, :half]), tiles(bc[:, half:])
    rin = lax.broadcasted_iota(jnp.int32, (1, SUBLANES, 1), 1)
    for lev in range(in_levels):
        s = 1 << lev
        mr = jnp.where(rin >= s, ap_ref[0, 2 * lev:2 * lev + 1, :][None], 0.0)
        mi = jnp.where(rin >= s, ap_ref[0, 2 * lev + 1:2 * lev + 2, :][None], 0.0)
        sr, si = pltpu.roll(lre, s, 1), pltpu.roll(lim, s, 1)
        lre, lim = lre + (mr * sr - mi * si), lim + (mr * si + mi * sr)
    lsh_re = jnp.where(rin >= 1, pltpu.roll(lre, 1, 1), 0.0).reshape(rows, half)
    lsh_im = jnp.where(rin >= 1, pltpu.roll(lim, 1, 1), 0.0).reshape(rows, half)
    lre, lim = lre.reshape(rows, half), lim.reshape(rows, half)
    nslab = half // LANES
    slabs = [slice(q * LANES, (q + 1) * LANES) for q in range(nslab)]
    for q in range(nslab):
        lre_ref[q] = lre[:, slabs[q]]
        lim_ref[q] = lim[:, slabs[q]]

    ends = pl.ds(SUBLANES - 1, n_grp, stride=SUBLANES)
    ere = jnp.concatenate([lre_ref[q, ends, :] for q in range(nslab)], axis=1)
    eim = jnp.concatenate([lim_ref[q, ends, :] for q in range(nslab)], axis=1)
    grp = lax.broadcasted_iota(jnp.int32, (n_grp, 1), 0)
    cre, cim = cre_ref[...], cim_ref[...]
    gr, gi = ap_ref[0, 2 * in_levels:2 * in_levels + 1, :], ap_ref[0, 2 * in_levels + 1:2 * in_levels + 2, :]
    ere = ere + jnp.where(grp == 0, gr * cre - gi * cim, 0.0)
    eim = eim + jnp.where(grp == 0, gr * cim + gi * cre, 0.0)
    for lev in range(in_levels, n_levels):
        ere, eim = scan_level(ere, eim, lev, 1 << (lev - in_levels), grp)
    pre = jnp.where(grp == 0, cre, pltpu.roll(ere, 1, 0))
    pim = jnp.where(grp == 0, cim, pltpu.roll(eim, 1, 0))
    cre_ref[...] = ere[n_grp - 1:n_grp, :]
    cim_ref[...] = eim[n_grp - 1:n_grp, :]

    for q in range(nslab):
        for r in range(SUBLANES):
            ebr_ref[q, pl.ds(r, n_grp, stride=SUBLANES), :] = pre[:, slabs[q]]
            ebi_ref[q, pl.ds(r, n_grp, stride=SUBLANES), :] = pim[:, slabs[q]]
    ebr = jnp.concatenate([ebr_ref[q] for q in range(nslab)], axis=1)
    ebi = jnp.concatenate([ebi_ref[q] for q in range(nslab)], axis=1)
    qr = jnp.concatenate([aq_ref[0, 0:SUBLANES, :]] * n_grp, axis=0)
    qi = jnp.concatenate([aq_ref[0, SUBLANES:2 * SUBLANES, :]] * n_grp, axis=0)
    st_re = lsh_re + (qr * ebr - qi * ebi)
    st_im = lsh_im + (qr * ebi + qi * ebr)
    sst = jnp.concatenate([st_re, st_im], axis=1).astype(jnp.bfloat16)
    y = y1 + jnp.dot(sst, wc_ref[...], preferred_element_type=jnp.float32)
    for t in range(CHUNK):
        y_ref[pl.ds(t, rows, stride=CHUNK), :] = y[:, t * LANES:(t + 1) * LANES]


def _ssm_chunk(u4, kb, fb, mb, ap, aq, w_f32, w_scale, casts, rows_per_seq):
    nblk, r, size = u4.shape
    n_levels = ap.shape[1] // 2
    assert SSM_ROWS == 1 << n_levels and rows_per_seq % SSM_ROWS == 0 and size == ROW_WIDTH
    nb = r // SSM_ROWS
    steps = nblk * nb
    wrows, wcols = w_f32.shape
    wblk = wcols // steps
    assert wblk * steps == wcols and wblk % LANES == 0
    cast_specs = []
    for w, _ in casts:
        rblk = w.shape[0] // steps
        assert rblk * steps == w.shape[0] and rblk % (2 * SUBLANES) == 0
        cast_specs.append(pl.BlockSpec((rblk, w.shape[1]), lambda g, i: (g * nb + i, 0)))
    kern = functools.partial(_ssm_chunk_kernel, rows=SSM_ROWS, n_levels=n_levels,
                             blocks_per_seq=rows_per_seq // SSM_ROWS,
                             cast_scales=tuple(s for _, s in casts))
    mat = pltpu.VMEM((ROW_WIDTH, ROW_WIDTH), jnp.bfloat16)
    state = pltpu.VMEM((BLOCK_STATES // LANES, SSM_ROWS, LANES), jnp.float32)
    assert 2 * BLOCK_STATES == ROW_WIDTH
    return pl.pallas_call(
        kern,
        grid=(nblk, r // SSM_ROWS),
        in_specs=[pl.BlockSpec((1, SSM_ROWS, size), lambda g, i: (g, i, 0)),
                  pl.BlockSpec((CHUNK, SSM_GROUP, LANES), lambda g, i: (0, 0, g)),
                  pl.BlockSpec((CHUNK, SSM_GROUP, BLOCK_STATES), lambda g, i: (0, 0, g)),
                  pl.BlockSpec((CHUNK, SSM_GROUP, BLOCK_STATES), lambda g, i: (0, 0, g)),
                  pl.BlockSpec((CHUNK, SSM_STATE, LANES), lambda g, i: (0, 0, g)),
                  pl.BlockSpec((CHUNK, SSM_STATE, LANES), lambda g, i: (0, 0, g)),
                  pl.BlockSpec((1,) + ap.shape[1:], lambda g, i: (g, 0, 0)),
                  pl.BlockSpec((1,) + aq.shape[1:], lambda g, i: (g, 0, 0)),
                  pl.BlockSpec((wrows, wblk), lambda g, i: (0, g * nb + i)),
                  pl.BlockSpec((1, wblk), lambda g, i: (0, g * nb + i))] + cast_specs,
        out_specs=[pl.BlockSpec((SSM_ROWS * CHUNK, LANES), lambda g, i: (i, g)),
                   pl.BlockSpec((wrows, wblk), lambda g, i: (0, g * nb + i))] + cast_specs,
        out_shape=[jax.ShapeDtypeStruct((r * CHUNK, nblk * LANES), jnp.float32),
                   jax.ShapeDtypeStruct((wrows, wcols), jnp.bfloat16)]
                  + [jax.ShapeDtypeStruct(w.shape, jnp.bfloat16) for w, _ in casts],
        scratch_shapes=[mat, mat, mat,
                        pltpu.VMEM((1, BLOCK_STATES), jnp.float32),
                        pltpu.VMEM((1, BLOCK_STATES), jnp.float32),
                        state, state, state, state],
        compiler_params=pltpu.CompilerParams(dimension_semantics=("arbitrary", "arbitrary"),
                                             vmem_limit_bytes=VMEM_LIMIT),
        name="ssm_chunk",
    )(u4, kb, *fb, *mb, ap, aq, w_f32, w_scale, *[w for w, _ in casts])


def _main_kernel(x_ref, y_ref, pre_g_ref, w_ref, cw_ref, cb_ref, lng_ref, lnb_ref, wco_ref,
                 wglu_ref, bglu_ref, wso_ref, wout_ref, post_g_ref, o_ref,
                 h_ref, glu_ref, cs_ref, cv_ref, pz_ref, so_ref, a_ref, acc_ref, yg_ref, yb_ref,
                 lin_ref, *, tile, d, ds):
    bf16, f32 = jnp.bfloat16, jnp.float32
    o_zc = 2 * d
    p_zs, p_gc, p_gs, p_end = d, d + ds, 2 * d + ds, 3 * d + ds
    nlane = d // LANES
    pieces = [slice(r, r + EW_ROWS) for r in range(0, tile, EW_ROWS)]

    @pl.when(pl.program_id(1) == 0)
    def _():
        cs_ref[:, 0:CONV_HALO, :] = jnp.zeros((nlane, CONV_HALO, LANES), f32)

    for rows in pieces:
        h_ref[rows, :] = (_rms_scale(x_ref[0, rows, :]) * pre_g_ref[...]).astype(bf16)

    glu_ref[...] = jnp.dot(h_ref[...], w_ref[:, 0:o_zc], preferred_element_type=f32)
    for rows in pieces:
        cu = glu_ref[rows, 0:d] * _sigmoid(glu_ref[rows, d:2 * d])
        for j in range(nlane):
            cs_ref[j, CONV_HALO + rows.start:CONV_HALO + rows.stop, :] = cu[:, j * LANES:(j + 1) * LANES]

    def ssm_glu():
        for rows in pieces:
            yg = _gelu_tanh(y_ref[0, rows, :])
            yg_ref[rows, :] = yg
            yb_ref[rows, :] = yg.astype(bf16)
        lin_ref[...] = jnp.dot(yb_ref[...], wglu_ref[...], preferred_element_type=f32)

    def ssm_out():
        for rows in pieces:
            gate = _sigmoid(lin_ref[rows, :] + bglu_ref[...]) * _silu(pz_ref[rows, p_zs:p_gc])
            yb_ref[rows, :] = (yg_ref[rows, :] * gate).astype(bf16)
        so_ref[...] = jnp.dot(yb_ref[...], wso_ref[...], preferred_element_type=f32)

    def conv_lane_block(j, lanes):
        for r in range(tile // CONV_ROWS):
            acc = jnp.broadcast_to(cb_ref[:, lanes], (CONV_ROWS, LANES))
            for lag in range(CONV_SIZE):
                lo = CONV_HALO + r * CONV_ROWS - lag
                tap = cw_ref[CONV_SIZE - 1 - lag:CONV_SIZE - lag, lanes]
                acc = acc + tap * cs_ref[j, lo:lo + CONV_ROWS, :]
            cv_ref[r * CONV_ROWS:(r + 1) * CONV_ROWS, lanes] = acc

    pz_ref[:, 0:p_zs] = jnp.dot(h_ref[...], w_ref[:, o_zc:o_zc + p_zs], preferred_element_type=f32)
    pz_ref[:, p_zs:p_end] = jnp.dot(h_ref[...], w_ref[:, o_zc + p_zs + ds:o_zc + p_end + ds],
                                    preferred_element_type=f32)
    ssm_glu()
    ssm_out()

    def conv_step(j, carry):
        conv_lane_block(j, pl.ds(pl.multiple_of(j * LANES, LANES), LANES))
        return carry

    lax.fori_loop(0, nlane, conv_step, 0)
    cs_ref[:, 0:CONV_HALO, :] = cs_ref[:, tile:CONV_HALO + tile, :]

    for rows in pieces:
        cv = cv_ref[rows, :]
        xc = cv - jnp.mean(cv, axis=-1, keepdims=True)
        var = jnp.mean(xc * xc, axis=-1, keepdims=True)
        ln = xc * lax.rsqrt(var + LN_EPS) * lng_ref[...] + lnb_ref[...]
        a_ref[rows, :] = (_silu(ln) * _silu(pz_ref[rows, 0:p_zs])).astype(bf16)
    acc_ref[...] = jnp.dot(a_ref[...], wco_ref[...], preferred_element_type=f32)

    for rows in pieces:
        merged = (_sigmoid(pz_ref[rows, p_gc:p_gs]) * acc_ref[rows, :]
                  + _sigmoid(pz_ref[rows, p_gs:p_end]) * so_ref[rows, :])
        a_ref[rows, :] = merged.astype(bf16)
    acc_ref[...] = jnp.dot(a_ref[...], wout_ref[...], preferred_element_type=f32)
    for rows in pieces:
        o_ref[0, rows, :] = x_ref[0, rows, :] + _rms_scale(acc_ref[rows, :]) * post_g_ref[...]


def _main(x, y, pre_g, w_in, cw, cb, lng, lnb, wco, wglu, bglu, wso, wout, post_g):
    b, l, d = x.shape
    ds = y.shape[-1]
    tile = MAIN_TILE
    assert CONV_HALO >= CONV_SIZE - 1 and CONV_HALO % SUBLANES == 0 and tile >= CONV_HALO
    kern = functools.partial(_main_kernel, tile=tile, d=d, ds=ds)
    consts = (pre_g, w_in, cw, cb, lng, lnb, wco, wglu, bglu, wso, wout, post_g)
    return pl.pallas_call(
        kern,
        grid=(b, l // tile),
        in_specs=[pl.BlockSpec((1, tile, d), lambda i, j: (i, j, 0)),
                  pl.BlockSpec((1, tile, ds), lambda i, j: (i, j, 0))]
                 + [_const_spec(a.shape) for a in consts],
        out_specs=pl.BlockSpec((1, tile, d), lambda i, j: (i, j, 0)),
        out_shape=jax.ShapeDtypeStruct((b, l, d), x.dtype),
        scratch_shapes=[pltpu.VMEM((tile, d), jnp.bfloat16),
                        pltpu.VMEM((tile, 2 * d), jnp.float32),
                        pltpu.VMEM((d // LANES, CONV_HALO + tile, LANES), jnp.float32),
                        pltpu.VMEM((tile, d), jnp.float32),
                        pltpu.VMEM((tile, 3 * d + ds), jnp.float32),
                        pltpu.VMEM((tile, d), jnp.float32),
                        pltpu.VMEM((tile, d), jnp.bfloat16),
                        pltpu.VMEM((tile, d), jnp.float32),
                        pltpu.VMEM((tile, ds), jnp.float32),
                        pltpu.VMEM((tile, ds), jnp.bfloat16),
                        pltpu.VMEM((tile, ds), jnp.float32)],
        compiler_params=pltpu.CompilerParams(dimension_semantics=("arbitrary", "arbitrary"),
                                             vmem_limit_bytes=VMEM_LIMIT),
        name="main_block",
    )(x, y, *consts)


def _layer(x, pre_g, w_in, conv_w, conv_b, ln_g, ln_b, w_conv_out, lam_re, lam_im, log_dt,
           b_re, b_im, c_re, c_im, d_skip, w_glu, b_glu, w_ssm_out, w_out, post_g):
    bsz, length, d = x.shape
    ds = w_glu.shape[0]
    o_us = 3 * d
    col = jnp.arange(w_in.shape[1])
    half_cols = ~((col < d) | ((col >= o_us) & (col < o_us + ds)))
    w_scale = jnp.where(half_cols, 0.5, 1.0).astype(w_in.dtype)
    row = lambda v: v.reshape(1, -1)
    assert lam_re.shape == (ds // SSM_GROUP, SSM_STATE)

    u4 = _us_proj(x.reshape(bsz * length, d), row(pre_g), w_in, o_us, ds)
    n_levels = SSM_ROWS.bit_length() - 1
    kb, fb, mb, ap, aq = _ssm_compact(lam_re, lam_im, log_dt, b_re, b_im, c_re, c_im, d_skip, n_levels)
    casts = [(w_conv_out, 1.0), (w_glu, 0.5), (w_ssm_out, 1.0), (w_out, 1.0)]
    y, w_bf, wco_bf, wglu_bf, wso_bf, wout_bf = _ssm_chunk(
        u4, kb, fb, mb, ap, aq, w_in, row(w_scale), casts, length // CHUNK)
    y = y.reshape(bsz, length, ds)

    cw = jnp.concatenate([conv_w, jnp.zeros((SUBLANES - CONV_SIZE % SUBLANES, d), conv_w.dtype)], axis=0)
    return _main(x, y, row(pre_g), w_bf, cw, row(conv_b), row(0.5 * ln_g), row(0.5 * ln_b),
                 wco_bf, wglu_bf, row(0.5 * b_glu), wso_bf, wout_bf, row(post_g))


def kernel(x, pre_norm_gain, w_in, conv_w, conv_b, conv_ln_gain, conv_ln_bias, w_conv_out, ssm_lambda_re, ssm_lambda_im, ssm_log_dt, ssm_b_re, ssm_b_im, ssm_c_re, ssm_c_im, ssm_d, w_ssm_glu, b_ssm_glu, w_ssm_out, w_out, post_norm_gain):
    for l in range(pre_norm_gain.shape[0]):
        x = _layer(x, pre_norm_gain[l], w_in[l], conv_w[l], conv_b[l], conv_ln_gain[l],
                   conv_ln_bias[l], w_conv_out[l], ssm_lambda_re[l], ssm_lambda_im[l],
                   ssm_log_dt[l], ssm_b_re[l], ssm_b_im[l], ssm_c_re[l], ssm_c_im[l], ssm_d[l],
                   w_ssm_glu[l], b_ssm_glu[l], w_ssm_out[l], w_out[l], post_norm_gain[l])
    return x
```

```python
import functools
import math

import jax
import jax.numpy as jnp
from jax import lax
from jax.experimental import pallas as pl
from jax.experimental.pallas import tpu as pltpu

RMS_EPS = 1e-6
LN_EPS = 1e-5
CONV_SIZE = 31
SSM_GROUP = 16
SSM_STATE = 64
CHUNK = 8
LANES = 128
SUBLANES = 8
GROUPS_PER_BLOCK = LANES // SSM_GROUP
BLOCK_STATES = GROUPS_PER_BLOCK * SSM_STATE
ROW_WIDTH = CHUNK * LANES

US_TILE = 2048
SSM_ROWS = 512
MAIN_TILE = 512
CONV_HALO = 32
CONV_ROWS = 64
MXU_COLS = 256
EW_ROWS = 16
VMEM_LIMIT = 56 * 1024 * 1024


def _sigmoid(h):
    return 0.5 * jnp.tanh(h) + 0.5


def _silu(h):
    return h * jnp.tanh(h) + h


def _gelu_tanh(x):
    c = math.sqrt(2.0 / math.pi)
    return 0.5 * x * (1.0 + jnp.tanh(c * (x + 0.044715 * (x * x * x))))


def _rms_scale(x):
    return x * lax.rsqrt(jnp.mean(x * x, axis=-1, keepdims=True) + RMS_EPS)


def _const_spec(shape):
    zeros = (0,) * len(shape)
    return pl.BlockSpec(shape, lambda *_: zeros)


def _us_proj_kernel(x_ref, g_ref, w_ref, u_ref, slab_ref, *, tile, nblk):
    h = _rms_scale(x_ref[...]) * g_ref[...]
    u = jnp.dot(h.astype(jnp.bfloat16), w_ref[...].astype(jnp.bfloat16),
                preferred_element_type=jnp.float32)
    for b in range(nblk):
        slab_ref[b] = u[:, b * LANES:(b + 1) * LANES]
    for b in range(nblk):
        for k in range(CHUNK):
            u_ref[b, :, k * LANES:(k + 1) * LANES] = slab_ref[
                b, pl.ds(k, tile // CHUNK, stride=CHUNK), :].astype(u_ref.dtype)


def _us_proj(x2, pre_g, w_in, col, n):
    t, d = x2.shape
    nblk = n // LANES
    assert col % n == 0
    kern = functools.partial(_us_proj_kernel, tile=US_TILE, nblk=nblk)
    return pl.pallas_call(
        kern,
        grid=(t // US_TILE,),
        in_specs=[pl.BlockSpec((US_TILE, d), lambda i: (i, 0)),
                  _const_spec((1, d)),
                  pl.BlockSpec((d, n), lambda i: (0, col // n))],
        out_specs=pl.BlockSpec((nblk, US_TILE // CHUNK, ROW_WIDTH), lambda i: (0, i, 0)),
        out_shape=jax.ShapeDtypeStruct((nblk, t // CHUNK, ROW_WIDTH), jnp.bfloat16),
        scratch_shapes=[pltpu.VMEM((nblk, US_TILE, LANES), jnp.float32)],
        compiler_params=pltpu.CompilerParams(dimension_semantics=("arbitrary",),
                                             vmem_limit_bytes=VMEM_LIMIT),
        name="us_proj",
    )(x2, pre_g, w_in)


def _ssm_compact(lam_re, lam_im, log_dt, b_re, b_im, c_re, c_im, d, n_levels):
    groups, states = lam_re.shape
    gh = groups * SSM_GROUP
    dt = jnp.exp(log_dt)[:, None]
    lr, li = lam_re, lam_im
    scan_pows = [CHUNK << lev for lev in range(n_levels)]
    row_pows = [CHUNK * r for r in range(SUBLANES)]
    nv = jnp.asarray(list(range(CHUNK + 1)) + scan_pows + row_pows, jnp.float32)[:, None, None]
    mag = jnp.exp(nv * (lr * dt))
    pw_r = mag * jnp.cos(nv * (li * dt))
    pw_i = mag * jnp.sin(nv * (li * dt))
    pr, pi = pw_r[:CHUNK + 1], pw_i[:CHUNK + 1]
    ar, ai = pr[1], pi[1]
    den = lr * lr + li * li
    zr = ((ar - 1.0) * lr + ai * li) / den
    zi = (ai * lr - (ar - 1.0) * li) / den
    bbr = zr[..., None] * b_re - zi[..., None] * b_im
    bbi = zr[..., None] * b_im + zi[..., None] * b_re

    bt_r = bbr.transpose(2, 0, 1).reshape(SSM_GROUP, groups * states)
    bt_i = bbi.transpose(2, 0, 1).reshape(SSM_GROUP, groups * states)
    qr = pr[CHUNK - 1::-1].reshape(CHUNK, groups * states)
    qi = pi[CHUNK - 1::-1].reshape(CHUNK, groups * states)
    fb = (bt_r, bt_i, qr, qi)

    spread = lambda v: jnp.repeat(jnp.swapaxes(v, -1, -2), SSM_GROUP, axis=-1)
    ct_r = c_re.transpose(2, 0, 1).reshape(states, gh)
    ct_i = c_im.transpose(2, 0, 1).reshape(states, gh)
    mb = (ct_r, ct_i, spread(pr[1:]), spread(pi[1:]))
    er = pr[:CHUNK, :, :, None] * bbr - pi[:CHUNK, :, :, None] * bbi
    ei = pr[:CHUNK, :, :, None] * bbi + pi[:CHUNK, :, :, None] * bbr
    hp = lax.Precision.HIGHEST
    kj = (jnp.einsum("gxp,jgpi->jgxi", c_re, er, precision=hp)
          - jnp.einsum("gxp,jgpi->jgxi", c_im, ei, precision=hp))
    kb = kj.transpose(0, 3, 1, 2).reshape(CHUNK, SSM_GROUP, gh)
    skip = (d.T[:, :, None] * jnp.eye(SSM_GROUP, dtype=d.dtype)[:, None, :]).reshape(SSM_GROUP, gh)
    kb = kb.at[0].add(skip)

    nblk = groups // GROUPS_PER_BLOCK
    lo, hi = CHUNK + 1, CHUNK + 1 + n_levels
    ap = jnp.stack([pw_r[lo:hi], pw_i[lo:hi]], axis=1)
    ap = ap.reshape(2 * n_levels, nblk, BLOCK_STATES).transpose(1, 0, 2)
    aq = jnp.concatenate([pw_r[hi:], pw_i[hi:]])
    aq = aq.reshape(2 * SUBLANES, nblk, BLOCK_STATES).transpose(1, 0, 2)
    return kb, fb, mb, ap, aq


def _block_diag(compact, group_rows, group_cols):
    n = compact.shape[1]
    tiled = jnp.concatenate([compact] * GROUPS_PER_BLOCK, axis=0)
    r = lax.broadcasted_iota(jnp.int32, tiled.shape, 0) // group_rows
    c = lax.broadcasted_iota(jnp.int32, tiled.shape, 1) // group_cols
    return jnp.where(r == c, tiled, 0.0).astype(jnp.bfloat16)


def _ssm_chunk_kernel(*refs, rows, n_levels, blocks_per_seq, cast_scales):
    n_cast = len(cast_scales)
    (u_ref, kb_ref, btr_ref, bti_ref, qr_ref, qi_ref, ctr_ref, cti_ref, pxr_ref, pxi_ref,
     ap_ref, aq_ref, wf_ref, ws_ref) = refs[:14]
    cast_in = refs[14:14 + n_cast]
    y_ref, wbf_ref = refs[14 + n_cast:16 + n_cast]
    cast_out = refs[16 + n_cast:16 + 2 * n_cast]
    (tz_ref, wb_ref, wc_ref, cre_ref, cim_ref, lre_ref, lim_ref, ebr_ref, ebi_ref) = refs[16 + 2 * n_cast:]
    half = BLOCK_STATES
    wbf_ref[...] = (wf_ref[...] * ws_ref[...]).astype(wbf_ref.dtype)
    for src, dst, scale in zip(cast_in, cast_out, cast_scales):
        dst[...] = (src[...] * scale).astype(dst.dtype)

    @pl.when(pl.program_id(1) == 0)
    def _():
        tz_ref[...] = jnp.zeros_like(tz_ref)
        for lag in range(CHUNK):
            blk = _block_diag(kb_ref[lag], SSM_GROUP, SSM_GROUP)
            for k in range(CHUNK - lag):
                t = k + lag
                tz_ref[k * LANES:(k + 1) * LANES, t * LANES:(t + 1) * LANES] = blk
        btr, bti, ctr, cti = btr_ref[...], bti_ref[...], ctr_ref[...], cti_ref[...]
        for k in range(CHUNK):
            qr, qi = qr_ref[k:k + 1, :], qi_ref[k:k + 1, :]
            pxr, pxi = pxr_ref[k], pxi_ref[k]
            fb = (qr * btr - qi * bti, qr * bti + qi * btr)
            mb = (ctr * pxr - cti * pxi, -(ctr * pxi + cti * pxr))
            for ri in range(2):
                wb_ref[k * LANES:(k + 1) * LANES, ri * half:(ri + 1) * half] = _block_diag(
                    fb[ri], SSM_GROUP, SSM_STATE)
                wc_ref[ri * half:(ri + 1) * half, k * LANES:(k + 1) * LANES] = _block_diag(
                    mb[ri], SSM_STATE, SSM_GROUP)

    @pl.when(pl.program_id(1) % blocks_per_seq == 0)
    def _():
        cre_ref[...] = jnp.zeros_like(cre_ref)
        cim_ref[...] = jnp.zeros_like(cim_ref)

    def scan_level(vr, vi, lev, shift, pos):
        mr, mi = ap_ref[0, 2 * lev:2 * lev + 1, :], ap_ref[0, 2 * lev + 1:2 * lev + 2, :]
        keep = pos >= shift
        sr = jnp.where(keep, pltpu.roll(vr, shift, 0), 0.0)
        si = jnp.where(keep, pltpu.roll(vi, shift, 0), 0.0)
        return vr + (mr * sr - mi * si), vi + (mr * si + mi * sr)

    u = u_ref[0]
    y1 = jnp.dot(u, tz_ref[...], preferred_element_type=jnp.float32)
    bc = jnp.dot(u, wb_ref[...], preferred_element_type=jnp.float32)
    n_grp = rows // SUBLANES
    in_levels = SUBLANES.bit_length() - 1
    assert n_grp == 1 << (n_levels - in_levels)

    tiles = lambda v: v.reshape(n_grp, SUBLANES, half)
    lre, lim = tiles(bc[:, :half]), tiles(bc[:, half:])
    rin = lax.broadcasted_iota(jnp.int32, (1, SUBLANES, 1), 1)
    for lev in range(in_levels):
        s = 1 << lev
        mr = jnp.where(rin >= s, ap_ref[0, 2 * lev:2 * lev + 1, :][None], 0.0)
        mi = jnp.where(rin >= s, ap_ref[0, 2 * lev + 1:2 * lev + 2, :][None], 0.0)
        sr, si = pltpu.roll(lre, s, 1), pltpu.roll(lim, s, 1)
        lre, lim = lre + (mr * sr - mi * si), lim + (mr * si + mi * sr)
    lsh_re = jnp.where(rin >= 1, pltpu.roll(lre, 1, 1), 0.0).reshape(rows, half)
    lsh_im = jnp.where(rin >= 1, pltpu.roll(lim, 1, 1), 0.0).reshape(rows, half)
    lre, lim = lre.reshape(rows, half), lim.reshape(rows, half)
    nslab = half // LANES
    slabs = [slice(q * LANES, (q + 1) * LANES) for q in range(nslab)]
    for q in range(nslab):
        lre_ref[q] = lre[:, slabs[q]]
        lim_ref[q] = lim[:, slabs[q]]

    ends = pl.ds(SUBLANES - 1, n_grp, stride=SUBLANES)
    ere = jnp.concatenate([lre_ref[q, ends, :] for q in range(nslab)], axis=1)
    eim = jnp.concatenate([lim_ref[q, ends, :] for q in range(nslab)], axis=1)
    grp = lax.broadcasted_iota(jnp.int32, (n_grp, 1), 0)
    cre, cim = cre_ref[...], cim_ref[...]
    gr, gi = ap_ref[0, 2 * in_levels:2 * in_levels + 1, :], ap_ref[0, 2 * in_levels + 1:2 * in_levels + 2, :]
    ere = ere + jnp.where(grp == 0, gr * cre - gi * cim, 0.0)
    eim = eim + jnp.where(grp == 0, gr * cim + gi * cre, 0.0)
    for lev in range(in_levels, n_levels):
        ere, eim = scan_level(ere, eim, lev, 1 << (lev - in_levels), grp)
    pre = jnp.where(grp == 0, cre, pltpu.roll(ere, 1, 0))
    pim = jnp.where(grp == 0, cim, pltpu.roll(eim, 1, 0))
    cre_ref[...] = ere[n_grp - 1:n_grp, :]
    cim_ref[...] = eim[n_grp - 1:n_grp, :]

    for q in range(nslab):
        for r in range(SUBLANES):
            ebr_ref[q, pl.ds(r, n_grp, stride=SUBLANES), :] = pre[:, slabs[q]]
            ebi_ref[q, pl.ds(r, n_grp, stride=SUBLANES), :] = pim[:, slabs[q]]
    ebr = jnp.concatenate([ebr_ref[q] for q in range(nslab)], axis=1)
    ebi = jnp.concatenate([ebi_ref[q] for q in range(nslab)], axis=1)
    qr = jnp.concatenate([aq_ref[0, 0:SUBLANES, :]] * n_grp, axis=0)
    qi = jnp.concatenate([aq_ref[0, SUBLANES:2 * SUBLANES, :]] * n_grp, axis=0)
    st_re = lsh_re + (qr * ebr - qi * ebi)
    st_im = lsh_im + (qr * ebi + qi * ebr)
    sst = jnp.concatenate([st_re, st_im], axis=1).astype(jnp.bfloat16)
    y = y1 + jnp.dot(sst, wc_ref[...], preferred_element_type=jnp.float32)
    for t in range(CHUNK):
        y_ref[pl.ds(t, rows, stride=CHUNK), :] = y[:, t * LANES:(t + 1) * LANES]


def _ssm_chunk(u4, kb, fb, mb, ap, aq, w_f32, w_scale, casts, rows_per_seq):
    nblk, r, size = u4.shape
    n_levels = ap.shape[1] // 2
    assert SSM_ROWS == 1 << n_levels and rows_per_seq % SSM_ROWS == 0 and size == ROW_WIDTH
    nb = r // SSM_ROWS
    steps = nblk * nb
    wrows, wcols = w_f32.shape
    wblk = wcols // steps
    assert wblk * steps == wcols and wblk % LANES == 0
    cast_specs = []
    for w, _ in casts:
        rblk = w.shape[0] // steps
        assert rblk * steps == w.shape[0] and rblk % (2 * SUBLANES) == 0
        cast_specs.append(pl.BlockSpec((rblk, w.shape[1]), lambda g, i: (g * nb + i, 0)))
    kern = functools.partial(_ssm_chunk_kernel, rows=SSM_ROWS, n_levels=n_levels,
                             blocks_per_seq=rows_per_seq // SSM_ROWS,
                             cast_scales=tuple(s for _, s in casts))
    mat = pltpu.VMEM((ROW_WIDTH, ROW_WIDTH), jnp.bfloat16)
    state = pltpu.VMEM((BLOCK_STATES // LANES, SSM_ROWS, LANES), jnp.float32)
    assert 2 * BLOCK_STATES == ROW_WIDTH
    return pl.pallas_call(
        kern,
        grid=(nblk, r // SSM_ROWS),
        in_specs=[pl.BlockSpec((1, SSM_ROWS, size), lambda g, i: (g, i, 0)),
                  pl.BlockSpec((CHUNK, SSM_GROUP, LANES), lambda g, i: (0, 0, g)),
                  pl.BlockSpec((SSM_GROUP, BLOCK_STATES), lambda g, i: (0, g)),
                  pl.BlockSpec((SSM_GROUP, BLOCK_STATES), lambda g, i: (0, g)),
                  pl.BlockSpec((CHUNK, BLOCK_STATES), lambda g, i: (0, g)),
                  pl.BlockSpec((CHUNK, BLOCK_STATES), lambda g, i: (0, g)),
                  pl.BlockSpec((SSM_STATE, LANES), lambda g, i: (0, g)),
                  pl.BlockSpec((SSM_STATE, LANES), lambda g, i: (0, g)),
                  pl.BlockSpec((CHUNK, SSM_STATE, LANES), lambda g, i: (0, 0, g)),
                  pl.BlockSpec((CHUNK, SSM_STATE, LANES), lambda g, i: (0, 0, g)),
                  pl.BlockSpec((1,) + ap.shape[1:], lambda g, i: (g, 0, 0)),
                  pl.BlockSpec((1,) + aq.shape[1:], lambda g, i: (g, 0, 0)),
                  pl.BlockSpec((wrows, wblk), lambda g, i: (0, g * nb + i)),
                  pl.BlockSpec((1, wblk), lambda g, i: (0, g * nb + i))] + cast_specs,
        out_specs=[pl.BlockSpec((SSM_ROWS * CHUNK, LANES), lambda g, i: (i, g)),
                   pl.BlockSpec((wrows, wblk), lambda g, i: (0, g * nb + i))] + cast_specs,
        out_shape=[jax.ShapeDtypeStruct((r * CHUNK, nblk * LANES), jnp.float32),
                   jax.ShapeDtypeStruct((wrows, wcols), jnp.bfloat16)]
                  + [jax.ShapeDtypeStruct(w.shape, jnp.bfloat16) for w, _ in casts],
        scratch_shapes=[mat, mat, mat,
                        pltpu.VMEM((1, BLOCK_STATES), jnp.float32),
                        pltpu.VMEM((1, BLOCK_STATES), jnp.float32),
                        state, state, state, state],
        compiler_params=pltpu.CompilerParams(dimension_semantics=("arbitrary", "arbitrary"),
                                             vmem_limit_bytes=VMEM_LIMIT),
        name="ssm_chunk",
    )(u4, kb, *fb, *mb, ap, aq, w_f32, w_scale, *[w for w, _ in casts])


def _main_kernel(x_ref, y_ref, pre_g_ref, w_ref, cw_ref, cb_ref, lng_ref, lnb_ref, wco_ref,
                 wglu_ref, bglu_ref, wso_ref, wout_ref, post_g_ref, o_ref,
                 h_ref, glu_ref, cs_ref, cv_ref, pz_ref, so_ref, a_ref, acc_ref, yg_ref, yb_ref,
                 lin_ref, *, tile, d, ds):
    bf16, f32 = jnp.bfloat16, jnp.float32
    o_zc = 2 * d
    p_zs, p_gc, p_gs, p_end = d, d + ds, 2 * d + ds, 3 * d + ds
    nlane = d // LANES
    pieces = [slice(r, r + EW_ROWS) for r in range(0, tile, EW_ROWS)]

    @pl.when(pl.program_id(1) == 0)
    def _():
        cs_ref[:, 0:CONV_HALO, :] = jnp.zeros((nlane, CONV_HALO, LANES), f32)

    for rows in pieces:
        h_ref[rows, :] = (_rms_scale(x_ref[0, rows, :]) * pre_g_ref[...]).astype(bf16)

    glu_ref[...] = jnp.dot(h_ref[...], w_ref[:, 0:o_zc], preferred_element_type=f32)
    for rows in pieces:
        cu = glu_ref[rows, 0:d] * _sigmoid(glu_ref[rows, d:2 * d])
        for j in range(nlane):
            cs_ref[j, CONV_HALO + rows.start:CONV_HALO + rows.stop, :] = cu[:, j * LANES:(j + 1) * LANES]

    def ssm_glu():
        for rows in pieces:
            yg = _gelu_tanh(y_ref[0, rows, :])
            yg_ref[rows, :] = yg
            yb_ref[rows, :] = yg.astype(bf16)
        lin_ref[...] = jnp.dot(yb_ref[...], wglu_ref[...], preferred_element_type=f32)

    def ssm_out():
        for rows in pieces:
            gate = _sigmoid(lin_ref[rows, :] + bglu_ref[...]) * _silu(pz_ref[rows, p_zs:p_gc])
            yb_ref[rows, :] = (yg_ref[rows, :] * gate).astype(bf16)
        so_ref[...] = jnp.dot(yb_ref[...], wso_ref[...], preferred_element_type=f32)

    def conv_lane_block(j, lanes):
        for r in range(tile // CONV_ROWS):
            acc = jnp.broadcast_to(cb_ref[:, lanes], (CONV_ROWS, LANES))
            for lag in range(CONV_SIZE):
                lo = CONV_HALO + r * CONV_ROWS - lag
                tap = cw_ref[CONV_SIZE - 1 - lag:CONV_SIZE - lag, lanes]
                acc = acc + tap * cs_ref[j, lo:lo + CONV_ROWS, :]
            cv_ref[r * CONV_ROWS:(r + 1) * CONV_ROWS, lanes] = acc

    pz_ref[:, 0:p_zs] = jnp.dot(h_ref[...], w_ref[:, o_zc:o_zc + p_zs], preferred_element_type=f32)
    pz_ref[:, p_zs:p_end] = jnp.dot(h_ref[...], w_ref[:, o_zc + p_zs + ds:o_zc + p_end + ds],
                                    preferred_element_type=f32)
    ssm_glu()
    ssm_out()

    def conv_step(j, carry):
        conv_lane_block(j, pl.ds(pl.multiple_of(j * LANES, LANES), LANES))
        return carry

    lax.fori_loop(0, nlane, conv_step, 0)
    cs_ref[:, 0:CONV_HALO, :] = cs_ref[:, tile:CONV_HALO + tile, :]

    for rows in pieces:
        cv = cv_ref[rows, :]
        xc = cv - jnp.mean(cv, axis=-1, keepdims=True)
        var = jnp.mean(xc * xc, axis=-1, keepdims=True)
        ln = xc * lax.rsqrt(var + LN_EPS) * lng_ref[...] + lnb_ref[...]
        a_ref[rows, :] = (_silu(ln) * _silu(pz_ref[rows, 0:p_zs])).astype(bf16)
    acc_ref[...] = jnp.dot(a_ref[...], wco_ref[...], preferred_element_type=f32)

    for rows in pieces:
        merged = (_sigmoid(pz_ref[rows, p_gc:p_gs]) * acc_ref[rows, :]
                  + _sigmoid(pz_ref[rows, p_gs:p_end]) * so_ref[rows, :])
        a_ref[rows, :] = merged.astype(bf16)
    acc_ref[...] = jnp.dot(a_ref[...], wout_ref[...], preferred_element_type=f32)
    for rows in pieces:
        o_ref[0, rows, :] = x_ref[0, rows, :] + _rms_scale(acc_ref[rows, :]) * post_g_ref[...]


def _main(x, y, pre_g, w_in, cw, cb, lng, lnb, wco, wglu, bglu, wso, wout, post_g):
    b, l, d = x.shape
    ds = y.shape[-1]
    tile = MAIN_TILE
    assert CONV_HALO >= CONV_SIZE - 1 and CONV_HALO % SUBLANES == 0 and tile >= CONV_HALO
    kern = functools.partial(_main_kernel, tile=tile, d=d, ds=ds)
    consts = (pre_g, w_in, cw, cb, lng, lnb, wco, wglu, bglu, wso, wout, post_g)
    return pl.pallas_call(
        kern,
        grid=(b, l // tile),
        in_specs=[pl.BlockSpec((1, tile, d), lambda i, j: (i, j, 0)),
                  pl.BlockSpec((1, tile, ds), lambda i, j: (i, j, 0))]
                 + [_const_spec(a.shape) for a in consts],
        out_specs=pl.BlockSpec((1, tile, d), lambda i, j: (i, j, 0)),
        out_shape=jax.ShapeDtypeStruct((b, l, d), x.dtype),
        scratch_shapes=[pltpu.VMEM((tile, d), jnp.bfloat16),
                        pltpu.VMEM((tile, 2 * d), jnp.float32),
                        pltpu.VMEM((d // LANES, CONV_HALO + tile, LANES), jnp.float32),
                        pltpu.VMEM((tile, d), jnp.float32),
                        pltpu.VMEM((tile, 3 * d + ds), jnp.float32),
                        pltpu.VMEM((tile, d), jnp.float32),
                        pltpu.VMEM((tile, d), jnp.bfloat16),
                        pltpu.VMEM((tile, d), jnp.float32),
                        pltpu.VMEM((tile, ds), jnp.float32),
                        pltpu.VMEM((tile, ds), jnp.bfloat16),
                        pltpu.VMEM((tile, ds), jnp.float32)],
        compiler_params=pltpu.CompilerParams(dimension_semantics=("arbitrary", "arbitrary"),
                                             vmem_limit_bytes=VMEM_LIMIT),
        name="main_block",
    )(x, y, *consts)


def _layer(x, pre_g, w_in, conv_w, conv_b, ln_g, ln_b, w_conv_out, lam_re, lam_im, log_dt,
           b_re, b_im, c_re, c_im, d_skip, w_glu, b_glu, w_ssm_out, w_out, post_g):
    bsz, length, d = x.shape
    ds = w_glu.shape[0]
    o_us = 3 * d
    col = jnp.arange(w_in.shape[1])
    half_cols = ~((col < d) | ((col >= o_us) & (col < o_us + ds)))
    w_scale = jnp.where(half_cols, 0.5, 1.0).astype(w_in.dtype)
    row = lambda v: v.reshape(1, -1)
    assert lam_re.shape == (ds // SSM_GROUP, SSM_STATE)

    u4 = _us_proj(x.reshape(bsz * length, d), row(pre_g), w_in, o_us, ds)
    n_levels = SSM_ROWS.bit_length() - 1
    kb, fb, mb, ap, aq = _ssm_compact(lam_re, lam_im, log_dt, b_re, b_im, c_re, c_im, d_skip, n_levels)
    casts = [(w_conv_out, 1.0), (w_glu, 0.5), (w_ssm_out, 1.0), (w_out, 1.0)]
    y, w_bf, wco_bf, wglu_bf, wso_bf, wout_bf = _ssm_chunk(
        u4, kb, fb, mb, ap, aq, w_in, row(w_scale), casts, length // CHUNK)
    y = y.reshape(bsz, length, ds)

    cw = jnp.concatenate([conv_w, jnp.zeros((SUBLANES - CONV_SIZE % SUBLANES, d), conv_w.dtype)], axis=0)
    return _main(x, y, row(pre_g), w_bf, cw, row(conv_b), row(0.5 * ln_g), row(0.5 * ln_b),
                 wco_bf, wglu_bf, row(0.5 * b_glu), wso_bf, wout_bf, row(post_g))


def kernel(x, pre_norm_gain, w_in, conv_w, conv_b, conv_ln_gain, conv_ln_bias, w_conv_out, ssm_lambda_re, ssm_lambda_im, ssm_log_dt, ssm_b_re, ssm_b_im, ssm_c_re, ssm_c_im, ssm_d, w_ssm_glu, b_ssm_glu, w_ssm_out, w_out, post_norm_gain):
    for l in range(pre_norm_gain.shape[0]):
        x = _layer(x, pre_norm_gain[l], w_in[l], conv_w[l], conv_b[l], conv_ln_gain[l],
                   conv_ln_bias[l], w_conv_out[l], ssm_lambda_re[l], ssm_lambda_im[l],
                   ssm_log_dt[l], ssm_b_re[l], ssm_b_im[l], ssm_c_re[l], ssm_c_im[l], ssm_d[l],
                   w_ssm_glu[l], b_ssm_glu[l], w_ssm_out[l], w_out[l], post_norm_gain[l])
    return x
```

```python
import functools
import math

import jax
import jax.numpy as jnp
from jax import lax
from jax.experimental import pallas as pl
from jax.experimental.pallas import tpu as pltpu

RMS_EPS = 1e-6
LN_EPS = 1e-5
CONV_SIZE = 31
SSM_GROUP = 16
SSM_STATE = 64
CHUNK = 8
LANES = 128
SUBLANES = 8
GROUPS_PER_BLOCK = LANES // SSM_GROUP
BLOCK_STATES = GROUPS_PER_BLOCK * SSM_STATE
ROW_WIDTH = CHUNK * LANES

US_TILE = 2048
SSM_ROWS = 512
MAIN_TILE = 512
CONV_HALO = 32
CONV_ROWS = 64
MXU_COLS = 256
EW_ROWS = 16
VMEM_LIMIT = 56 * 1024 * 1024


def _sigmoid(h):
    return 0.5 * jnp.tanh(h) + 0.5


def _silu(h):
    return h * jnp.tanh(h) + h


def _gelu_tanh(x):
    c = math.sqrt(2.0 / math.pi)
    return 0.5 * x * (1.0 + jnp.tanh(c * (x + 0.044715 * (x * x * x))))


def _rms_scale(x):
    return x * lax.rsqrt(jnp.mean(x * x, axis=-1, keepdims=True) + RMS_EPS)


def _const_spec(shape):
    zeros = (0,) * len(shape)
    return pl.BlockSpec(shape, lambda *_: zeros)


def _us_proj_kernel(x_ref, g_ref, w_ref, u_ref, slab_ref, *, tile, nblk):
    h = _rms_scale(x_ref[...]) * g_ref[...]
    u = jnp.dot(h.astype(jnp.bfloat16), w_ref[...].astype(jnp.bfloat16),
                preferred_element_type=jnp.float32)
    for b in range(nblk):
        slab_ref[b] = u[:, b * LANES:(b + 1) * LANES]
    for b in range(nblk):
        for k in range(CHUNK):
            u_ref[b, :, k * LANES:(k + 1) * LANES] = slab_ref[
                b, pl.ds(k, tile // CHUNK, stride=CHUNK), :].astype(u_ref.dtype)


def _us_proj(x2, pre_g, w_in, col, n):
    t, d = x2.shape
    nblk = n // LANES
    assert col % n == 0
    kern = functools.partial(_us_proj_kernel, tile=US_TILE, nblk=nblk)
    return pl.pallas_call(
        kern,
        grid=(t // US_TILE,),
        in_specs=[pl.BlockSpec((US_TILE, d), lambda i: (i, 0)),
                  _const_spec((1, d)),
                  pl.BlockSpec((d, n), lambda i: (0, col // n))],
        out_specs=pl.BlockSpec((nblk, US_TILE // CHUNK, ROW_WIDTH), lambda i: (0, i, 0)),
        out_shape=jax.ShapeDtypeStruct((nblk, t // CHUNK, ROW_WIDTH), jnp.bfloat16),
        scratch_shapes=[pltpu.VMEM((nblk, US_TILE, LANES), jnp.float32)],
        compiler_params=pltpu.CompilerParams(dimension_semantics=("arbitrary",),
                                             vmem_limit_bytes=VMEM_LIMIT),
        name="us_proj",
    )(x2, pre_g, w_in)


def _ssm_compact(lam_re, lam_im, log_dt, b_re, b_im, c_re, c_im, d, n_levels):
    groups, states = lam_re.shape
    gh = groups * SSM_GROUP
    dt = jnp.exp(log_dt)[:, None]
    lr, li = lam_re, lam_im
    scan_pows = [CHUNK << lev for lev in range(n_levels)]
    row_pows = [CHUNK * r for r in range(SUBLANES)]
    nv = jnp.asarray(list(range(CHUNK + 1)) + scan_pows + row_pows, jnp.float32)[:, None, None]
    mag = jnp.exp(nv * (lr * dt))
    pw_r = mag * jnp.cos(nv * (li * dt))
    pw_i = mag * jnp.sin(nv * (li * dt))
    pr, pi = pw_r[:CHUNK + 1], pw_i[:CHUNK + 1]
    ar, ai = pr[1], pi[1]
    den = lr * lr + li * li
    zr = ((ar - 1.0) * lr + ai * li) / den
    zi = (ai * lr - (ar - 1.0) * li) / den
    bbr = zr[..., None] * b_re - zi[..., None] * b_im
    bbi = zr[..., None] * b_im + zi[..., None] * b_re

    bt_r = bbr.transpose(2, 0, 1).reshape(1, SSM_GROUP, groups * states)
    bt_i = bbi.transpose(2, 0, 1).reshape(1, SSM_GROUP, groups * states)
    qr = pr[CHUNK - 1::-1].reshape(CHUNK, 1, groups * states)
    qi = pi[CHUNK - 1::-1].reshape(CHUNK, 1, groups * states)
    fb = (qr * bt_r - qi * bt_i, qr * bt_i + qi * bt_r)

    spread = lambda v: jnp.repeat(jnp.swapaxes(v, -1, -2), SSM_GROUP, axis=-1)
    px_r, px_i = spread(pr), spread(pi)
    ct_r = c_re.transpose(2, 0, 1).reshape(1, states, gh)
    ct_i = c_im.transpose(2, 0, 1).reshape(1, states, gh)
    m_r = ct_r * px_r - ct_i * px_i
    m_i = ct_r * px_i + ct_i * px_r
    mb = (m_r[1:], -m_i[1:])
    er = pr[:CHUNK, :, :, None] * bbr - pi[:CHUNK, :, :, None] * bbi
    ei = pr[:CHUNK, :, :, None] * bbi + pi[:CHUNK, :, :, None] * bbr
    hp = lax.Precision.HIGHEST
    kj = (jnp.einsum("gxp,jgpi->jgxi", c_re, er, precision=hp)
          - jnp.einsum("gxp,jgpi->jgxi", c_im, ei, precision=hp))
    kb = kj.transpose(0, 3, 1, 2).reshape(CHUNK, SSM_GROUP, gh)
    skip = (d.T[:, :, None] * jnp.eye(SSM_GROUP, dtype=d.dtype)[:, None, :]).reshape(SSM_GROUP, gh)
    kb = kb.at[0].add(skip)

    nblk = groups // GROUPS_PER_BLOCK
    lo, hi = CHUNK + 1, CHUNK + 1 + n_levels
    ap = jnp.stack([pw_r[lo:hi], pw_i[lo:hi]], axis=1)
    ap = ap.reshape(2 * n_levels, nblk, BLOCK_STATES).transpose(1, 0, 2)
    aq = jnp.concatenate([pw_r[hi:], pw_i[hi:]])
    aq = aq.reshape(2 * SUBLANES, nblk, BLOCK_STATES).transpose(1, 0, 2)
    return kb, fb, mb, ap, aq


def _block_diag(compact, group_rows, group_cols):
    n = compact.shape[1]
    tiled = jnp.concatenate([compact] * GROUPS_PER_BLOCK, axis=0)
    r = lax.broadcasted_iota(jnp.int32, tiled.shape, 0) // group_rows
    c = lax.broadcasted_iota(jnp.int32, tiled.shape, 1) // group_cols
    return jnp.where(r == c, tiled, 0.0).astype(jnp.bfloat16)


def _ssm_chunk_kernel(*refs, rows, n_levels, blocks_per_seq, cast_scales):
    n_cast = len(cast_scales)
    (u_ref, kb_ref, fbr_ref, fbi_ref, mbr_ref, mbi_ref, ap_ref, aq_ref, wf_ref, ws_ref) = refs[:10]
    cast_in = refs[10:10 + n_cast]
    y_ref, wbf_ref = refs[10 + n_cast:12 + n_cast]
    cast_out = refs[12 + n_cast:12 + 2 * n_cast]
    (tw_ref, wb_ref, cre_ref, cim_ref, lre_ref, lim_ref, ebr_ref, ebi_ref) = refs[12 + 2 * n_cast:]
    tz_ref = tw_ref.at[0:ROW_WIDTH, :]
    wc_ref = tw_ref.at[ROW_WIDTH:2 * ROW_WIDTH, :]
    half = BLOCK_STATES
    wbf_ref[...] = (wf_ref[...] * ws_ref[...]).astype(wbf_ref.dtype)
    for src, dst, scale in zip(cast_in, cast_out, cast_scales):
        dst[...] = (src[...] * scale).astype(dst.dtype)

    @pl.when(pl.program_id(1) == 0)
    def _():
        tz_ref[...] = jnp.zeros_like(tz_ref)
        for lag in range(CHUNK):
            blk = _block_diag(kb_ref[lag], SSM_GROUP, SSM_GROUP)
            for k in range(CHUNK - lag):
                t = k + lag
                tz_ref[k * LANES:(k + 1) * LANES, t * LANES:(t + 1) * LANES] = blk
        for ri, (fb_ref, mb_ref) in enumerate(((fbr_ref, mbr_ref), (fbi_ref, mbi_ref))):
            for k in range(CHUNK):
                wb_ref[k * LANES:(k + 1) * LANES, ri * half:(ri + 1) * half] = _block_diag(
                    fb_ref[k], SSM_GROUP, SSM_STATE)
                wc_ref[ri * half:(ri + 1) * half, k * LANES:(k + 1) * LANES] = _block_diag(
                    mb_ref[k], SSM_STATE, SSM_GROUP)

    @pl.when(pl.program_id(1) % blocks_per_seq == 0)
    def _():
        cre_ref[...] = jnp.zeros_like(cre_ref)
        cim_ref[...] = jnp.zeros_like(cim_ref)

    def scan_level(vr, vi, lev, shift, pos):
        mr, mi = ap_ref[0, 2 * lev:2 * lev + 1, :], ap_ref[0, 2 * lev + 1:2 * lev + 2, :]
        keep = pos >= shift
        sr = jnp.where(keep, pltpu.roll(vr, shift, 0), 0.0)
        si = jnp.where(keep, pltpu.roll(vi, shift, 0), 0.0)
        return vr + (mr * sr - mi * si), vi + (mr * si + mi * sr)

    u = u_ref[0]
    bc = jnp.dot(u, wb_ref[...], preferred_element_type=jnp.float32)
    n_grp = rows // SUBLANES
    in_levels = SUBLANES.bit_length() - 1
    assert n_grp == 1 << (n_levels - in_levels)

    tiles = lambda v: v.reshape(n_grp, SUBLANES, half)
    lre, lim = tiles(bc[:, :half]), tiles(bc[:, half:])
    rin = lax.broadcasted_iota(jnp.int32, (1, SUBLANES, 1), 1)
    for lev in range(in_levels):
        s = 1 << lev
        mr = jnp.where(rin >= s, ap_ref[0, 2 * lev:2 * lev + 1, :][None], 0.0)
        mi = jnp.where(rin >= s, ap_ref[0, 2 * lev + 1:2 * lev + 2, :][None], 0.0)
        sr, si = pltpu.roll(lre, s, 1), pltpu.roll(lim, s, 1)
        lre, lim = lre + (mr * sr - mi * si), lim + (mr * si + mi * sr)
    lsh_re = jnp.where(rin >= 1, pltpu.roll(lre, 1, 1), 0.0).reshape(rows, half)
    lsh_im = jnp.where(rin >= 1, pltpu.roll(lim, 1, 1), 0.0).reshape(rows, half)
    lre, lim = lre.reshape(rows, half), lim.reshape(rows, half)
    nslab = half // LANES
    slabs = [slice(q * LANES, (q + 1) * LANES) for q in range(nslab)]
    for q in range(nslab):
        lre_ref[q] = lre[:, slabs[q]]
        lim_ref[q] = lim[:, slabs[q]]

    ends = pl.ds(SUBLANES - 1, n_grp, stride=SUBLANES)
    ere = jnp.concatenate([lre_ref[q, ends, :] for q in range(nslab)], axis=1)
    eim = jnp.concatenate([lim_ref[q, ends, :] for q in range(nslab)], axis=1)
    grp = lax.broadcasted_iota(jnp.int32, (n_grp, 1), 0)
    cre, cim = cre_ref[...], cim_ref[...]
    gr, gi = ap_ref[0, 2 * in_levels:2 * in_levels + 1, :], ap_ref[0, 2 * in_levels + 1:2 * in_levels + 2, :]
    ere = ere + jnp.where(grp == 0, gr * cre - gi * cim, 0.0)
    eim = eim + jnp.where(grp == 0, gr * cim + gi * cre, 0.0)
    for lev in range(in_levels, n_levels):
        ere, eim = scan_level(ere, eim, lev, 1 << (lev - in_levels), grp)
    pre = jnp.where(grp == 0, cre, pltpu.roll(ere, 1, 0))
    pim = jnp.where(grp == 0, cim, pltpu.roll(eim, 1, 0))
    cre_ref[...] = ere[n_grp - 1:n_grp, :]
    cim_ref[...] = eim[n_grp - 1:n_grp, :]

    for q in range(nslab):
        for r in range(SUBLANES):
            ebr_ref[q, pl.ds(r, n_grp, stride=SUBLANES), :] = pre[:, slabs[q]]
            ebi_ref[q, pl.ds(r, n_grp, stride=SUBLANES), :] = pim[:, slabs[q]]
    ebr = jnp.concatenate([ebr_ref[q] for q in range(nslab)], axis=1)
    ebi = jnp.concatenate([ebi_ref[q] for q in range(nslab)], axis=1)
    qr = jnp.concatenate([aq_ref[0, 0:SUBLANES, :]] * n_grp, axis=0)
    qi = jnp.concatenate([aq_ref[0, SUBLANES:2 * SUBLANES, :]] * n_grp, axis=0)
    st_re = lsh_re + (qr * ebr - qi * ebi)
    st_im = lsh_im + (qr * ebi + qi * ebr)
    lhs = jnp.concatenate([u, st_re.astype(jnp.bfloat16), st_im.astype(jnp.bfloat16)], axis=1)
    y = jnp.dot(lhs, tw_ref[...], preferred_element_type=jnp.float32)
    for t in range(CHUNK):
        y_ref[pl.ds(t, rows, stride=CHUNK), :] = y[:, t * LANES:(t + 1) * LANES]


def _ssm_chunk(u4, kb, fb, mb, ap, aq, w_f32, w_scale, casts, rows_per_seq):
    nblk, r, size = u4.shape
    n_levels = ap.shape[1] // 2
    assert SSM_ROWS == 1 << n_levels and rows_per_seq % SSM_ROWS == 0 and size == ROW_WIDTH
    nb = r // SSM_ROWS
    steps = nblk * nb
    wrows, wcols = w_f32.shape
    wblk = wcols // steps
    assert wblk * steps == wcols and wblk % LANES == 0
    cast_specs = []
    for w, _ in casts:
        rblk = w.shape[0] // steps
        assert rblk * steps == w.shape[0] and rblk % (2 * SUBLANES) == 0
        cast_specs.append(pl.BlockSpec((rblk, w.shape[1]), lambda g, i: (g * nb + i, 0)))
    kern = functools.partial(_ssm_chunk_kernel, rows=SSM_ROWS, n_levels=n_levels,
                             blocks_per_seq=rows_per_seq // SSM_ROWS,
                             cast_scales=tuple(s for _, s in casts))
    mat = pltpu.VMEM((ROW_WIDTH, ROW_WIDTH), jnp.bfloat16)
    state = pltpu.VMEM((BLOCK_STATES // LANES, SSM_ROWS, LANES), jnp.float32)
    assert 2 * BLOCK_STATES == ROW_WIDTH
    return pl.pallas_call(
        kern,
        grid=(nblk, r // SSM_ROWS),
        in_specs=[pl.BlockSpec((1, SSM_ROWS, size), lambda g, i: (g, i, 0)),
                  pl.BlockSpec((CHUNK, SSM_GROUP, LANES), lambda g, i: (0, 0, g)),
                  pl.BlockSpec((CHUNK, SSM_GROUP, BLOCK_STATES), lambda g, i: (0, 0, g)),
                  pl.BlockSpec((CHUNK, SSM_GROUP, BLOCK_STATES), lambda g, i: (0, 0, g)),
                  pl.BlockSpec((CHUNK, SSM_STATE, LANES), lambda g, i: (0, 0, g)),
                  pl.BlockSpec((CHUNK, SSM_STATE, LANES), lambda g, i: (0, 0, g)),
                  pl.BlockSpec((1,) + ap.shape[1:], lambda g, i: (g, 0, 0)),
                  pl.BlockSpec((1,) + aq.shape[1:], lambda g, i: (g, 0, 0)),
                  pl.BlockSpec((wrows, wblk), lambda g, i: (0, g * nb + i)),
                  pl.BlockSpec((1, wblk), lambda g, i: (0, g * nb + i))] + cast_specs,
        out_specs=[pl.BlockSpec((SSM_ROWS * CHUNK, LANES), lambda g, i: (i, g)),
                   pl.BlockSpec((wrows, wblk), lambda g, i: (0, g * nb + i))] + cast_specs,
        out_shape=[jax.ShapeDtypeStruct((r * CHUNK, nblk * LANES), jnp.float32),
                   jax.ShapeDtypeStruct((wrows, wcols), jnp.bfloat16)]
                  + [jax.ShapeDtypeStruct(w.shape, jnp.bfloat16) for w, _ in casts],
        scratch_shapes=[pltpu.VMEM((2 * ROW_WIDTH, ROW_WIDTH), jnp.bfloat16), mat,
                        pltpu.VMEM((1, BLOCK_STATES), jnp.float32),
                        pltpu.VMEM((1, BLOCK_STATES), jnp.float32),
                        state, state, state, state],
        compiler_params=pltpu.CompilerParams(dimension_semantics=("arbitrary", "arbitrary"),
                                             vmem_limit_bytes=VMEM_LIMIT),
        name="ssm_chunk",
    )(u4, kb, *fb, *mb, ap, aq, w_f32, w_scale, *[w for w, _ in casts])


def _main_kernel(x_ref, y_ref, pre_g_ref, w_ref, cw_ref, cb_ref, lng_ref, lnb_ref, wco_ref,
                 wglu_ref, bglu_ref, wso_ref, wout_ref, post_g_ref, o_ref,
                 h_ref, glu_ref, cs_ref, cv_ref, pz_ref, so_ref, a_ref, acc_ref, yg_ref, yb_ref,
                 lin_ref, *, tile, d, ds):
    bf16, f32 = jnp.bfloat16, jnp.float32
    o_zc = 2 * d
    p_zs, p_gc, p_gs, p_end = d, d + ds, 2 * d + ds, 3 * d + ds
    nlane = d // LANES
    pieces = [slice(r, r + EW_ROWS) for r in range(0, tile, EW_ROWS)]

    @pl.when(pl.program_id(1) == 0)
    def _():
        cs_ref[:, 0:CONV_HALO, :] = jnp.zeros((nlane, CONV_HALO, LANES), f32)

    for rows in pieces:
        h_ref[rows, :] = (_rms_scale(x_ref[0, rows, :]) * pre_g_ref[...]).astype(bf16)

    glu_ref[...] = jnp.dot(h_ref[...], w_ref[:, 0:o_zc], preferred_element_type=f32)
    for rows in pieces:
        cu = glu_ref[rows, 0:d] * _sigmoid(glu_ref[rows, d:2 * d])
        for j in range(nlane):
            cs_ref[j, CONV_HALO + rows.start:CONV_HALO + rows.stop, :] = cu[:, j * LANES:(j + 1) * LANES]

    def ssm_glu():
        for rows in pieces:
            yg = _gelu_tanh(y_ref[0, rows, :])
            yg_ref[rows, :] = yg
            yb_ref[rows, :] = yg.astype(bf16)
        lin_ref[...] = jnp.dot(yb_ref[...], wglu_ref[...], preferred_element_type=f32)

    def ssm_out():
        for rows in pieces:
            gate = _sigmoid(lin_ref[rows, :] + bglu_ref[...]) * _silu(pz_ref[rows, p_zs:p_gc])
            yb_ref[rows, :] = (yg_ref[rows, :] * gate).astype(bf16)
        so_ref[...] = jnp.dot(yb_ref[...], wso_ref[...], preferred_element_type=f32)

    def conv_lane_block(j, lanes):
        for r in range(tile // CONV_ROWS):
            acc = jnp.broadcast_to(cb_ref[:, lanes], (CONV_ROWS, LANES))
            for lag in range(CONV_SIZE):
                lo = CONV_HALO + r * CONV_ROWS - lag
                tap = cw_ref[CONV_SIZE - 1 - lag:CONV_SIZE - lag, lanes]
                acc = acc + tap * cs_ref[j, lo:lo + CONV_ROWS, :]
            cv_ref[r * CONV_ROWS:(r + 1) * CONV_ROWS, lanes] = acc

    pz_ref[:, 0:p_zs] = jnp.dot(h_ref[...], w_ref[:, o_zc:o_zc + p_zs], preferred_element_type=f32)
    pz_ref[:, p_zs:p_end] = jnp.dot(h_ref[...], w_ref[:, o_zc + p_zs + ds:o_zc + p_end + ds],
                                    preferred_element_type=f32)
    ssm_glu()
    ssm_out()

    def conv_step(j, carry):
        conv_lane_block(j, pl.ds(pl.multiple_of(j * LANES, LANES), LANES))
        return carry

    lax.fori_loop(0, nlane, conv_step, 0)
    cs_ref[:, 0:CONV_HALO, :] = cs_ref[:, tile:CONV_HALO + tile, :]

    for rows in pieces:
        cv = cv_ref[rows, :]
        xc = cv - jnp.mean(cv, axis=-1, keepdims=True)
        var = jnp.mean(xc * xc, axis=-1, keepdims=True)
        ln = xc * lax.rsqrt(var + LN_EPS) * lng_ref[...] + lnb_ref[...]
        a_ref[rows, :] = (_silu(ln) * _silu(pz_ref[rows, 0:p_zs])).astype(bf16)
    acc_ref[...] = jnp.dot(a_ref[...], wco_ref[...], preferred_element_type=f32)

    for rows in pieces:
        merged = (_sigmoid(pz_ref[rows, p_gc:p_gs]) * acc_ref[rows, :]
                  + _sigmoid(pz_ref[rows, p_gs:p_end]) * so_ref[rows, :])
        a_ref[rows, :] = merged.astype(bf16)
    acc_ref[...] = jnp.dot(a_ref[...], wout_ref[...], preferred_element_type=f32)
    for rows in pieces:
        o_ref[0, rows, :] = x_ref[0, rows, :] + _rms_scale(acc_ref[rows, :]) * post_g_ref[...]


def _main(x, y, pre_g, w_in, cw, cb, lng, lnb, wco, wglu, bglu, wso, wout, post_g):
    b, l, d = x.shape
    ds = y.shape[-1]
    tile = MAIN_TILE
    assert CONV_HALO >= CONV_SIZE - 1 and CONV_HALO % SUBLANES == 0 and tile >= CONV_HALO
    kern = functools.partial(_main_kernel, tile=tile, d=d, ds=ds)
    consts = (pre_g, w_in, cw, cb, lng, lnb, wco, wglu, bglu, wso, wout, post_g)
    return pl.pallas_call(
        kern,
        grid=(b, l // tile),
        in_specs=[pl.BlockSpec((1, tile, d), lambda i, j: (i, j, 0)),
                  pl.BlockSpec((1, tile, ds), lambda i, j: (i, j, 0))]
                 + [_const_spec(a.shape) for a in consts],
        out_specs=pl.BlockSpec((1, tile, d), lambda i, j: (i, j, 0)),
        out_shape=jax.ShapeDtypeStruct((b, l, d), x.dtype),
        scratch_shapes=[pltpu.VMEM((tile, d), jnp.bfloat16),
                        pltpu.VMEM((tile, 2 * d), jnp.float32),
                        pltpu.VMEM((d // LANES, CONV_HALO + tile, LANES), jnp.float32),
                        pltpu.VMEM((tile, d), jnp.float32),
                        pltpu.VMEM((tile, 3 * d + ds), jnp.float32),
                        pltpu.VMEM((tile, d), jnp.float32),
                        pltpu.VMEM((tile, d), jnp.bfloat16),
                        pltpu.VMEM((tile, d), jnp.float32),
                        pltpu.VMEM((tile, ds), jnp.float32),
                        pltpu.VMEM((tile, ds), jnp.bfloat16),
                        pltpu.VMEM((tile, ds), jnp.float32)],
        compiler_params=pltpu.CompilerParams(dimension_semantics=("arbitrary", "arbitrary"),
                                             vmem_limit_bytes=VMEM_LIMIT),
        name="main_block",
    )(x, y, *consts)


def _layer(x, pre_g, w_in, conv_w, conv_b, ln_g, ln_b, w_conv_out, lam_re, lam_im, log_dt,
           b_re, b_im, c_re, c_im, d_skip, w_glu, b_glu, w_ssm_out, w_out, post_g):
    bsz, length, d = x.shape
    ds = w_glu.shape[0]
    o_us = 3 * d
    col = jnp.arange(w_in.shape[1])
    half_cols = ~((col < d) | ((col >= o_us) & (col < o_us + ds)))
    w_scale = jnp.where(half_cols, 0.5, 1.0).astype(w_in.dtype)
    row = lambda v: v.reshape(1, -1)
    assert lam_re.shape == (ds // SSM_GROUP, SSM_STATE)

    u4 = _us_proj(x.reshape(bsz * length, d), row(pre_g), w_in, o_us, ds)
    n_levels = SSM_ROWS.bit_length() - 1
    kb, fb, mb, ap, aq = _ssm_compact(lam_re, lam_im, log_dt, b_re, b_im, c_re, c_im, d_skip, n_levels)
    casts = [(w_conv_out, 1.0), (w_glu, 0.5), (w_ssm_out, 1.0), (w_out, 1.0)]
    y, w_bf, wco_bf, wglu_bf, wso_bf, wout_bf = _ssm_chunk(
        u4, kb, fb, mb, ap, aq, w_in, row(w_scale), casts, length // CHUNK)
    y = y.reshape(bsz, length, ds)

    cw = jnp.concatenate([conv_w, jnp.zeros((SUBLANES - CONV_SIZE % SUBLANES, d), conv_w.dtype)], axis=0)
    return _main(x, y, row(pre_g), w_bf, cw, row(conv_b), row(0.5 * ln_g), row(0.5 * ln_b),
                 wco_bf, wglu_bf, row(0.5 * b_glu), wso_bf, wout_bf, row(post_g))


def kernel(x, pre_norm_gain, w_in, conv_w, conv_b, conv_ln_gain, conv_ln_bias, w_conv_out, ssm_lambda_re, ssm_lambda_im, ssm_log_dt, ssm_b_re, ssm_b_im, ssm_c_re, ssm_c_im, ssm_d, w_ssm_glu, b_ssm_glu, w_ssm_out, w_out, post_norm_gain):
    for l in range(pre_norm_gain.shape[0]):
        x = _layer(x, pre_norm_gain[l], w_in[l], conv_w[l], conv_b[l], conv_ln_gain[l],
                   conv_ln_bias[l], w_conv_out[l], ssm_lambda_re[l], ssm_lambda_im[l],
                   ssm_log_dt[l], ssm_b_re[l], ssm_b_im[l], ssm_c_re[l], ssm_c_im[l], ssm_d[l],
                   w_ssm_glu[l], b_ssm_glu[l], w_ssm_out[l], w_out[l], post_norm_gain[l])
    return x
```
